```python
import jax, jax.numpy as jnp
from jax import lax
import numpy as np

D_MODEL = 1024
BATCH = 8
SEQ = 2048
DEPTH = 4
DEC_BATCH = 128
DEC_SEQ = 8
PAST_LEN = 16384
PAGE_SIZE = 128

D_CONV = D_MODEL // 2
CONV_W = 3
D_GMLP = D_MODEL // 2
G_HEADS = 8
G_HEAD_DIM = D_GMLP // G_HEADS
CHUNK = 128
D_POOL = D_MODEL // 2
POOL_WINDOWS = (2, 4, 8, 16)
POOL_GROUPS = len(POOL_WINDOWS)
POOL_GDIM = D_POOL // POOL_GROUPS
POOL_OUT_GDIM = D_MODEL // POOL_GROUPS
MAX_WIN = max(POOL_WINDOWS)
D_FF = 2816
N_SUB = 3
EPS = 1e-6
IN_SIZES = (D_CONV, D_CONV, D_CONV, D_GMLP, D_GMLP, D_POOL, D_MODEL, D_MODEL, D_MODEL)
IN_COLS = sum(IN_SIZES)
IN_SPLITS = tuple(int(s) for s in np.cumsum(IN_SIZES)[:-1])

kernel_name = 'hybrid_conv_gmlp_pool_decoder_step'


def _rmsnorm(x, g):
    xf = x.astype(jnp.float32)
    y = xf * lax.rsqrt(jnp.mean(xf * xf, axis=-1, keepdims=True) + EPS)
    return (y * g.astype(jnp.float32)).astype(x.dtype)


def _layernorm(x, g):
    xf = x.astype(jnp.float32)
    mu = jnp.mean(xf, axis=-1, keepdims=True)
    var = jnp.mean(jnp.square(xf - mu), axis=-1, keepdims=True)
    return ((xf - mu) * lax.rsqrt(var + EPS) * g.astype(jnp.float32)).astype(x.dtype)


def _modulate(h, shift, scale):
    return h * (1 + scale[:, None, :]) + shift[:, None, :]


def _swiglu(h, w_gu, w_dn):
    a, b = jnp.split(h @ w_gu, 2, axis=-1)
    return (jax.nn.silu(a) * b) @ w_dn


def _short_conv(ext, conv_w, L):
    return sum(ext[:, k:k + L] * conv_w[k] for k in range(CONV_W))


def _spatial_gate(v, w_s, b_s):
    bsz, L, _ = v.shape
    w = w_s * jnp.tril(jnp.ones((CHUNK, CHUNK), w_s.dtype))
    bias = b_s.T
    if L < CHUNK:
        vh = v.reshape(bsz, L, G_HEADS, G_HEAD_DIM)
        out = jnp.einsum('hts,bshd->bthd', w[:, :L, :L], vh) + bias[None, :L, :, None]
        return out.reshape(bsz, L, D_GMLP)
    n = -(-L // CHUNK)
    vp = jnp.pad(v, ((0, 0), (0, n * CHUNK - L), (0, 0))).reshape(bsz, n, CHUNK, G_HEADS, G_HEAD_DIM)
    out = jnp.einsum('hts,bnshd->bnthd', w, vp) + bias[None, None, :, :, None]
    return out.reshape(bsz, n * CHUNK, D_GMLP)[:, :L]


def _multi_pool(ext, L, pos0):
    hist = ext.shape[1] - L
    ef = ext.astype(jnp.float32)
    cs = jnp.concatenate([jnp.zeros_like(ef[:, :1]), jnp.cumsum(ef, axis=1)], axis=1)
    pos = pos0 + jnp.arange(L, dtype=jnp.int32)
    outs = []
    for gi, w in enumerate(POOL_WINDOWS):
        cg = cs[:, :, gi * POOL_GDIM:(gi + 1) * POOL_GDIM]
        s = cg[:, hist + 1:hist + 1 + L] - cg[:, hist + 1 - w:hist + 1 - w + L]
        cnt = jnp.minimum(pos + 1, w).astype(jnp.float32)
        outs.append(s / cnt[None, :, None])
    return jnp.concatenate(outs, axis=-1).astype(ext.dtype)


def _mixer(h, hist_conv, hist_pool, pos0, w_in, conv_w, w_out_a, ln_g, w_s, b_s, w_out_b,
           pool_w, pool_scale, w_o):
    bsz, L, _ = h.shape
    xa, bg, cg, u, v, p, ga, gb, gc = jnp.split(h @ w_in, IN_SPLITS, axis=-1)
    ext_c = jnp.concatenate([hist_conv, cg * xa], axis=1)
    y_a = (bg * _short_conv(ext_c, conv_w, L)) @ w_out_a
    new_conv = ext_c[:, -(CONV_W - 1):]
    u = jax.nn.gelu(u, approximate=False)
    v = _layernorm(jax.nn.gelu(v, approximate=False), ln_g)
    y_b = (u * _spatial_gate(v, w_s, b_s)) @ w_out_b
    v_open = v[:, (L // CHUNK) * CHUNK:]
    ext_p = jnp.concatenate([hist_pool, p], axis=1)
    pooled = (_multi_pool(ext_p, L, pos0) - p).reshape(bsz, L, POOL_GROUPS, POOL_GDIM)
    y_c = jnp.einsum('blgc,gcd->blgd', pooled, pool_w).reshape(bsz, L, D_MODEL) * pool_scale
    new_pool = ext_p[:, -(MAX_WIN - 1):]
    merged = jax.nn.sigmoid(ga) * y_a + jax.nn.sigmoid(gb) * y_b + jax.nn.sigmoid(gc) * y_c
    return merged @ w_o, new_conv, new_pool, v_open


def _layer(x, c, hist_conv, hist_pool, pos0, norm_g, w_ada, b_ada, w1_gu, w1_dn, w2_gu, w2_dn,
           w_in, conv_w, w_out_a, ln_g, w_s, b_s, w_out_b, pool_w, pool_scale, w_o):
    bsz = x.shape[0]
    mod = (jax.nn.silu(c) @ w_ada + b_ada).reshape(bsz, N_SUB, 3, D_MODEL)
    h = _modulate(_rmsnorm(x, norm_g[0]), mod[:, 0, 0], mod[:, 0, 1])
    x = x + 0.5 * mod[:, 0, 2][:, None, :] * _swiglu(h, w1_gu, w1_dn)
    h = _modulate(_rmsnorm(x, norm_g[1]), mod[:, 1, 0], mod[:, 1, 1])
    m, new_conv, new_pool, v_open = _mixer(h, hist_conv, hist_pool, pos0, w_in, conv_w, w_out_a,
                                           ln_g, w_s, b_s, w_out_b, pool_w, pool_scale, w_o)
    x = x + mod[:, 1, 2][:, None, :] * m
    h = _modulate(_rmsnorm(x, norm_g[2]), mod[:, 2, 0], mod[:, 2, 1])
    x = x + 0.5 * mod[:, 2, 2][:, None, :] * _swiglu(h, w2_gu, w2_dn)
    return x, new_conv, new_pool, v_open


def setup_inputs(seed: int = 0) -> dict:
    key = jax.random.key(seed)
    ks = jax.random.split(key, 32)
    f32 = jnp.float32

    def nrm(k, shape, scale):
        return jax.random.normal(k, shape, f32) * scale

    return {
        'x_prompt': nrm(ks[0], (BATCH, SEQ, D_MODEL), 1.0),
        'x_sample': nrm(ks[1], (DEC_BATCH, DEC_SEQ, D_MODEL), 1.0),
        'state_conv': nrm(ks[2], (DEPTH, DEC_BATCH, CONV_W - 1, D_CONV), 1.0),
        'state_pool': nrm(ks[3], (DEPTH, DEC_BATCH, MAX_WIN - 1, D_POOL), 1.0),
        'c_prompt': nrm(ks[4], (BATCH, D_MODEL), 1.0),
        'c_sample': nrm(ks[5], (DEC_BATCH, D_MODEL), 1.0),
        'norm_g': 1.0 + nrm(ks[6], (DEPTH, N_SUB, D_MODEL), 0.05),
        'w_ada': nrm(ks[7], (DEPTH, D_MODEL, N_SUB * 3 * D_MODEL), 0.5 * D_MODEL ** -0.5),
        'b_ada': nrm(ks[8], (DEPTH, N_SUB * 3 * D_MODEL), 0.02),
        'w1_gu': nrm(ks[9], (DEPTH, D_MODEL, 2 * D_FF), D_MODEL ** -0.5),
        'w1_dn': nrm(ks[10], (DEPTH, D_FF, D_MODEL), D_FF ** -0.5),
        'w2_gu': nrm(ks[11], (DEPTH, D_MODEL, 2 * D_FF), D_MODEL ** -0.5),
        'w2_dn': nrm(ks[12], (DEPTH, D_FF, D_MODEL), D_FF ** -0.5),
        'w_in': nrm(ks[13], (DEPTH, D_MODEL, IN_COLS), D_MODEL ** -0.5),
        'conv_w': nrm(ks[14], (DEPTH, CONV_W, D_CONV), CONV_W ** -0.5),
        'w_out_a': nrm(ks[15], (DEPTH, D_CONV, D_MODEL), D_CONV ** -0.5),
        'ln_g': 1.0 + nrm(ks[16], (DEPTH, D_GMLP), 0.05),
        'w_s': nrm(ks[17], (DEPTH, G_HEADS, CHUNK, CHUNK), CHUNK ** -0.5),
        'b_s': 1.0 + nrm(ks[18], (DEPTH, G_HEADS, CHUNK), 0.1),
        'w_out_b': nrm(ks[19], (DEPTH, D_GMLP, D_MODEL), D_GMLP ** -0.5),
        'pool_w': nrm(ks[20], (DEPTH, POOL_GROUPS, POOL_GDIM, POOL_OUT_GDIM), POOL_GDIM ** -0.5),
        'pool_scale': 1.0 + nrm(ks[21], (DEPTH, D_MODEL), 0.1),
        'w_o': nrm(ks[22], (DEPTH, D_MODEL, D_MODEL), D_MODEL ** -0.5),
        'final_norm_g': 1.0 + nrm(ks[23], (D_MODEL,), 0.05),
    }


def reference(x_prompt, x_sample, state_conv, state_pool, c_prompt, c_sample, norm_g, w_ada, b_ada,
              w1_gu, w1_dn, w2_gu, w2_dn, w_in, conv_w, w_out_a, ln_g, w_s, b_s, w_out_b,
              pool_w, pool_scale, w_o, final_norm_g):
    xp, xs = x_prompt, x_sample
    bp = xp.shape[0]
    zero_conv = jnp.zeros((bp, CONV_W - 1, D_CONV), xp.dtype)
    zero_pool = jnp.zeros((bp, MAX_WIN - 1, D_POOL), xp.dtype)
    conv_p, conv_s, pool_p, pool_s, v_s = [], [], [], [], []
    for l in range(DEPTH):
        prm = (norm_g[l], w_ada[l], b_ada[l], w1_gu[l], w1_dn[l], w2_gu[l], w2_dn[l], w_in[l],
               conv_w[l], w_out_a[l], ln_g[l], w_s[l], b_s[l], w_out_b[l], pool_w[l],
               pool_scale[l], w_o[l])
        xp, ncp, npp, _ = _layer(xp, c_prompt, zero_conv, zero_pool, 0, *prm)
        xs, ncs, nps, vs = _layer(xs, c_sample, state_conv[l], state_pool[l], PAST_LEN, *prm)
        conv_p.append(ncp)
        pool_p.append(npp)
        conv_s.append(ncs)
        pool_s.append(nps)
        v_s.append(vs)
    y_prompt = _rmsnorm(xp, final_norm_g)
    y_sample = _rmsnorm(xs, final_norm_g)
    new_conv_prompt = jnp.stack(conv_p)
    new_conv_sample = jnp.stack(conv_s)
    new_pool_prompt = jnp.stack(pool_p)
    new_pool_sample = jnp.stack(pool_s)
    new_gmlp_v_sample = jnp.stack(v_s)
    return (y_prompt, y_sample, new_conv_prompt, new_conv_sample, new_pool_prompt, new_pool_sample, new_gmlp_v_sample)
```

```python
import functools
import math

import jax
import jax.numpy as jnp
from jax import lax
from jax.experimental import pallas as pl
from jax.experimental.pallas import tpu as pltpu

F32 = jnp.float32
BF16 = jnp.bfloat16

EPS = 1e-6
N_SUB = 3
CONV_W = 3
G_HEADS = 8
CHUNK = 128
POOL_WINDOWS = (2, 4, 8, 16)
MAX_WIN = max(POOL_WINDOWS)
PAST_LEN = 16384

LANES = 128
SUBLANES = 8
TOKEN_TILE = 512
FF_CHUNKS = 2
VMEM_LIMIT = 56 * 1024 * 1024


def _flat(a):
    return a.reshape(-1, a.shape[-1])


def _rms_mod(x, g, shift, scale):
    ms = jnp.mean(x * x, axis=-1, keepdims=True)
    y = x * lax.rsqrt(ms + EPS) * g
    return y * (1.0 + scale) + shift


def _dot(a, b):
    return jnp.dot(a, b, preferred_element_type=F32)


def _gelu(x):
    return 0.5 * x * (1.0 + lax.erf(x * (1.0 / math.sqrt(2.0))))


def _ada_kernel(c_ref, w_ref, b_ref, o_ref):
    c = c_ref[...]
    sc = (c * jax.nn.sigmoid(c)).astype(BF16)
    o_ref[...] = _dot(sc, w_ref[...].astype(BF16)) + b_ref[...]


def _ada_call(c_all, w_ada, b_ada):
    depth, d, cols = w_ada.shape
    n_mod = cols // d
    rows = c_all.shape[0]
    return pl.pallas_call(
        _ada_kernel,
        grid=(depth, n_mod),
        in_specs=[
            pl.BlockSpec((rows, d), lambda l, j: (0, 0)),
            pl.BlockSpec((None, d, d), lambda l, j: (l, 0, j)),
            pl.BlockSpec((None, None, 1, d), lambda l, j: (l, j, 0, 0)),
        ],
        out_specs=pl.BlockSpec((None, None, rows, d), lambda l, j: (l, j, 0, 0)),
        out_shape=jax.ShapeDtypeStruct((depth, n_mod, rows, d), F32),
        compiler_params=pltpu.CompilerParams(
            dimension_semantics=("arbitrary", "arbitrary"), vmem_limit_bytes=VMEM_LIMIT),
        name="ada_mod",
    )(c_all, w_ada, b_ada.reshape(depth, n_mod, 1, d))


def _ffn_kernel(x_ref, mod_ref, ng_ref, wgu_ref, wdn_ref, fg_ref, o_ref, *, final_norm):
    x = x_ref[...]
    h = _rms_mod(x, ng_ref[...], mod_ref[0:1], mod_ref[1:2])
    hb = _flat(h).astype(BF16)
    n_chunks, _, two_ch = wgu_ref.shape
    ch = two_ch // 2
    acc = None
    for c in range(n_chunks):
        ab = _dot(hb, wgu_ref[c])
        a = ab[:, :ch]
        b = ab[:, ch:]
        g = (a * jax.nn.sigmoid(a)) * b
        part = _dot(g.astype(BF16), wdn_ref[c])
        acc = part if acc is None else acc + part
    y = x + (0.5 * mod_ref[2:3]) * acc.reshape(x.shape)
    if final_norm:
        ms = jnp.mean(y * y, axis=-1, keepdims=True)
        y = y * lax.rsqrt(ms + EPS) * fg_ref[...]
    o_ref[...] = y


def _const_spec(shape, index_map):
    return pl.BlockSpec(shape, index_map, pipeline_mode=pl.Buffered(1))


def _ffn_call(x, mod, ng, wgu, wdn, fg, *, l, sub, sample, final_norm):
    d = x.shape[-1]
    _, n_chunks, _, two_ch = wgu.shape
    if sample:
        steps, batch, _ = x.shape
        bt = TOKEN_TILE // steps
        grid = (batch // bt,)
        x_spec = pl.BlockSpec((steps, bt, d), lambda i: (0, i, 0))
        mod_spec = pl.BlockSpec((None, N_SUB, bt, d), lambda i: (l, sub, i, 0))
        c2 = lambda i: (l, sub, 0, 0)
        c4 = lambda i: (l, 0, 0, 0)
        c0 = lambda i: (0, 0)
    else:
        batch, seq, _ = x.shape
        grid = (batch, seq // TOKEN_TILE)
        x_spec = pl.BlockSpec((None, TOKEN_TILE, d), lambda b, t: (b, t, 0))
        mod_spec = pl.BlockSpec((None, None, None, N_SUB, d), lambda b, t: (l, b, sub, 0, 0))
        c2 = lambda b, t: (l, sub, 0, 0)
        c4 = lambda b, t: (l, 0, 0, 0)
        c0 = lambda b, t: (0, 0)
    return pl.pallas_call(
        functools.partial(_ffn_kernel, final_norm=final_norm),
        grid=grid,
        in_specs=[
            x_spec,
            mod_spec,
            pl.BlockSpec((None, None, 1, d), c2),
            _const_spec((None, n_chunks, d, two_ch), c4),
            _const_spec((None, n_chunks, two_ch // 2, d), c4),
            pl.BlockSpec((1, d), c0),
        ],
        out_specs=x_spec,
        out_shape=jax.ShapeDtypeStruct(x.shape, F32),
        compiler_params=pltpu.CompilerParams(
            dimension_semantics=("arbitrary",) * len(grid), vmem_limit_bytes=VMEM_LIMIT),
        name=("ffn_sample" if sample else "ffn_prompt"),
    )(x, mod, ng, wgu, wdn, fg)


def _mixer_core(x, mod_ref, ng_ref, win_ref, woa_ref, lng_ref, wob_ref, poolw_ref, pscale_ref,
                wo_ref, conv_fn, gate_fn, pool_fn):
    d_conv = woa_ref.shape[0]
    d_gmlp = wob_ref.shape[0]
    d_pool = poolw_ref.shape[0] * poolw_ref.shape[1]
    d_model = wo_ref.shape[0]
    h = _rms_mod(x, ng_ref[...], mod_ref[0:1], mod_ref[1:2])
    hb = _flat(h).astype(BF16)

    o0 = 0
    pa = _dot(hb, win_ref[:, o0:o0 + 3 * d_conv])
    xa = pa[:, :d_conv]
    bg = pa[:, d_conv:2 * d_conv]
    cg = pa[:, 2 * d_conv:]
    conv = conv_fn(cg * xa)
    y_a = _dot((bg * conv).astype(BF16), woa_ref[...])
    o0 += 3 * d_conv

    puv = _dot(hb, win_ref[:, o0:o0 + 2 * d_gmlp])
    u = _gelu(puv[:, :d_gmlp])
    v = _gelu(puv[:, d_gmlp:])
    mu = jnp.mean(v, axis=-1, keepdims=True)
    vc = v - mu
    var = jnp.mean(vc * vc, axis=-1, keepdims=True)
    v = vc * lax.rsqrt(var + EPS) * lng_ref[...]
    sg = gate_fn(v)
    y_b = _dot((u * sg).astype(BF16), wob_ref[...])
    o0 += 2 * d_gmlp

    p = _dot(hb, win_ref[:, o0:o0 + d_pool])
    diff = (pool_fn(p) - p).astype(BF16)
    gdim = poolw_ref.shape[1]
    y_c = jnp.concatenate(
        [_dot(diff[:, g * gdim:(g + 1) * gdim], poolw_ref[g]) for g in range(poolw_ref.shape[0])],
        axis=1) * pscale_ref[...]
    o0 += d_pool

    pg = _dot(hb, win_ref[:, o0:o0 + 3 * d_model])
    merged = (jax.nn.sigmoid(pg[:, :d_model]) * y_a
              + jax.nn.sigmoid(pg[:, d_model:2 * d_model]) * y_b
              + jax.nn.sigmoid(pg[:, 2 * d_model:]) * y_c)
    m = _dot(merged.astype(BF16), wo_ref[...])
    return x + mod_ref[2:3] * m.reshape(x.shape)


def _mixer_prompt_kernel(x_ref, mod_ref, ng_ref, win_ref, cw_ref, woa_ref, lng_ref, wpair_ref,
                         sgb_ref, wob_ref, poolw_ref, pscale_ref, wo_ref,
                         o_ref, ctail_ref, ptail_ref, zc_ref, pc_ref):
    t = pl.program_id(1)
    tm = x_ref.shape[0]

    @pl.when(t == 0)
    def _():
        zc_ref[...] = jnp.zeros_like(zc_ref)
        pc_ref[...] = jnp.zeros_like(pc_ref)

    def conv_fn(z):
        hist = zc_ref.shape[0]
        ext = jnp.concatenate([zc_ref[...], z], axis=0)
        z1 = pltpu.roll(ext, 1, 0)[hist:]
        z2 = pltpu.roll(ext, 2, 0)[hist:]
        tail = z[tm - hist:]
        zc_ref[...] = tail
        ctail_ref[...] = tail
        return z2 * cw_ref[0:1] + z1 * cw_ref[1:2] + z * cw_ref[2:3]

    def gate_fn(v):
        lane = lax.broadcasted_iota(jnp.int32, (CHUNK, LANES), 1)
        low = lane < (LANES // 2)
        n_blk = v.shape[1] // LANES
        rows = []
        for c in range(tm // CHUNK):
            vc = v[c * CHUNK:(c + 1) * CHUNK]
            outs = []
            for j in range(n_blk):
                blk = vc[:, j * LANES:(j + 1) * LANES]
                rhs = jnp.concatenate(
                    [jnp.where(low, blk, 0.0), jnp.where(low, 0.0, blk)], axis=0).astype(BF16)
                outs.append(_dot(wpair_ref[j], rhs))
            rows.append(jnp.concatenate(outs, axis=1) + sgb_ref[...])
        return jnp.concatenate(rows, axis=0)

    def pool_fn(p):
        hist = pc_ref.shape[0]
        ext = jnp.concatenate([pc_ref[...], p], axis=0)
        pos = t * tm + lax.broadcasted_iota(jnp.int32, (tm, 1), 0)
        outs = []
        for gi, w in enumerate(POOL_WINDOWS):
            s = ext[:, gi * LANES:(gi + 1) * LANES]
            k = 1
            while k < w:
                s = s + pltpu.roll(s, k, 0)
                k *= 2
            inv = 1.0 / jnp.minimum(pos + 1, w).astype(F32)
            outs.append(s[hist:] * inv)
        tail = p[tm - hist:]
        pc_ref[...] = tail
        ptail_ref[...] = tail
        return jnp.concatenate(outs, axis=1)

    o_ref[...] = _mixer_core(x_ref[...], mod_ref, ng_ref, win_ref, woa_ref, lng_ref, wob_ref,
                             poolw_ref, pscale_ref, wo_ref, conv_fn, gate_fn, pool_fn)


def _mixer_sample_kernel(x_ref, mod_ref, ng_ref, win_ref, cw_ref, woa_ref, lng_ref, sgc_ref,
                         sgb_ref, wob_ref, poolw_ref, pscale_ref, wo_ref, hc_ref, hp_ref,
                         o_ref, nconv_ref, npool_ref, vopen_ref):
    steps, bt, _ = x_ref.shape

    def conv_fn(z):
        z3 = z.reshape(steps, bt, z.shape[-1])
        ext = jnp.concatenate([hc_ref[...], z3], axis=0)
        nconv_ref[...] = ext[steps:]
        conv = sum(ext[k:k + steps] * cw_ref[k:k + 1] for k in range(CONV_W))
        return _flat(conv)

    def gate_fn(v):
        v3 = v.reshape(steps, bt, v.shape[-1])
        vopen_ref[...] = v3
        outs = []
        for tt in range(steps):
            acc = sgb_ref[tt:tt + 1] + sgc_ref[tt * steps:tt * steps + 1] * v3[0]
            for s in range(1, tt + 1):
                acc = acc + sgc_ref[tt * steps + s:tt * steps + s + 1] * v3[s]
            outs.append(acc)
        return _flat(jnp.stack(outs, axis=0))

    def pool_fn(p):
        p3 = p.reshape(steps, bt, p.shape[-1])
        ext = jnp.concatenate([hp_ref[...], p3], axis=0)
        hist = hp_ref.shape[0]
        npool_ref[...] = ext[steps:]
        outs = []
        for gi, w in enumerate(POOL_WINDOWS):
            s = ext[:, :, gi * LANES:(gi + 1) * LANES]
            k = 1
            while k < w:
                s = s[k:] + s[:-k]
                k *= 2
            first = hist - (w - 1)
            outs.append(jnp.stack(
                [s[first + tt] * (1.0 / min(PAST_LEN + tt + 1, w)) for tt in range(steps)], axis=0))
        return _flat(jnp.concatenate(outs, axis=-1))

    o_ref[...] = _mixer_core(x_ref[...], mod_ref, ng_ref, win_ref, woa_ref, lng_ref, wob_ref,
                             poolw_ref, pscale_ref, wo_ref, conv_fn, gate_fn, pool_fn)


def _mixer_prompt_call(x, mod, ng, win, cw, woa, lng, wpair, sgb, wob, poolw, pscale, wo, *, l):
    batch, seq, d = x.shape
    d_conv = cw.shape[-1]
    d_pool = poolw.shape[2] * poolw.shape[1]
    sub = 1
    tm = TOKEN_TILE
    cl = lambda *tail: (lambda b, t: (l,) + tail)
    full = lambda a: _const_spec((None,) + a.shape[1:], cl(*([0] * (a.ndim - 1))))
    return pl.pallas_call(
        _mixer_prompt_kernel,
        grid=(batch, seq // tm),
        in_specs=[
            pl.BlockSpec((None, tm, d), lambda b, t: (b, t, 0)),
            pl.BlockSpec((None, None, None, N_SUB, d), lambda b, t: (l, b, sub, 0, 0)),
            pl.BlockSpec((None, None, 1, d), lambda b, t: (l, sub, 0, 0)),
            full(win), full(cw), full(woa), full(lng), full(wpair), full(sgb), full(wob),
            full(poolw), full(pscale), full(wo),
        ],
        out_specs=[
            pl.BlockSpec((None, tm, d), lambda b, t: (b, t, 0)),
            pl.BlockSpec((None, SUBLANES, d_conv), lambda b, t: (b, 0, 0)),
            pl.BlockSpec((None, MAX_WIN, d_pool), lambda b, t: (b, 0, 0)),
        ],
        out_shape=[
            jax.ShapeDtypeStruct(x.shape, F32),
            jax.ShapeDtypeStruct((batch, SUBLANES, d_conv), F32),
            jax.ShapeDtypeStruct((batch, MAX_WIN, d_pool), F32),
        ],
        scratch_shapes=[pltpu.VMEM((SUBLANES, d_conv), F32), pltpu.VMEM((MAX_WIN, d_pool), F32)],
        compiler_params=pltpu.CompilerParams(
            dimension_semantics=("arbitrary", "arbitrary"), vmem_limit_bytes=VMEM_LIMIT),
        name="mixer_prompt",
    )(x, mod, ng, win, cw, woa, lng, wpair, sgb, wob, poolw, pscale, wo)


def _mixer_sample_call(x, mod, ng, win, cw, woa, lng, sgc, sgb, wob, poolw, pscale, wo,
                       hconv, hpool, *, l):
    steps, batch, d = x.shape
    d_conv = cw.shape[-1]
    d_gmlp = wob.shape[1]
    d_pool = poolw.shape[2] * poolw.shape[1]
    sub = 1
    bt = TOKEN_TILE // steps
    cl = lambda *tail: (lambda i: (l,) + tail)
    full = lambda a: _const_spec((None,) + a.shape[1:], cl(*([0] * (a.ndim - 1))))
    n_hc, n_hp = hconv.shape[1], hpool.shape[1]
    return pl.pallas_call(
        _mixer_sample_kernel,
        grid=(batch // bt,),
        in_specs=[
            pl.BlockSpec((steps, bt, d), lambda i: (0, i, 0)),
            pl.BlockSpec((None, N_SUB, bt, d), lambda i: (l, sub, i, 0)),
            pl.BlockSpec((None, None, 1, d), lambda i: (l, sub, 0, 0)),
            full(win), full(cw), full(woa), full(lng), full(sgc), full(sgb), full(wob),
            full(poolw), full(pscale), full(wo),
            pl.BlockSpec((None, n_hc, bt, d_conv), lambda i: (l, 0, i, 0)),
            pl.BlockSpec((None, n_hp, bt, d_pool), lambda i: (l, 0, i, 0)),
        ],
        out_specs=[
            pl.BlockSpec((steps, bt, d), lambda i: (0, i, 0)),
            pl.BlockSpec((n_hc, bt, d_conv), lambda i: (0, i, 0)),
            pl.BlockSpec((n_hp, bt, d_pool), lambda i: (0, i, 0)),
            pl.BlockSpec((steps, bt, d_gmlp), lambda i: (0, i, 0)),
        ],
        out_shape=[
            jax.ShapeDtypeStruct(x.shape, F32),
            jax.ShapeDtypeStruct((n_hc, batch, d_conv), F32),
            jax.ShapeDtypeStruct((n_hp, batch, d_pool), F32),
            jax.ShapeDtypeStruct((steps, batch, d_gmlp), F32),
        ],
        compiler_params=pltpu.CompilerParams(
            dimension_semantics=("arbitrary",), vmem_limit_bytes=VMEM_LIMIT),
        name="mixer_sample",
    )(x, mod, ng, win, cw, woa, lng, sgc, sgb, wob, poolw, pscale, wo, hconv, hpool)


def _chunk_ffn_weights(w_gu, w_dn):
    depth, d, two_ff = w_gu.shape
    d_ff = two_ff // 2
    ch = d_ff // FF_CHUNKS
    a = w_gu[:, :, :d_ff].reshape(depth, d, FF_CHUNKS, ch)
    b = w_gu[:, :, d_ff:].reshape(depth, d, FF_CHUNKS, ch)
    wgu = jnp.concatenate([a, b], axis=-1).transpose(0, 2, 1, 3).astype(BF16)
    wdn = w_dn.reshape(depth, FF_CHUNKS, ch, d).astype(BF16)
    return wgu, wdn


def kernel(x_prompt, x_sample, state_conv, state_pool, c_prompt, c_sample, norm_g, w_ada, b_ada,
           w1_gu, w1_dn, w2_gu, w2_dn, w_in, conv_w, w_out_a, ln_g, w_s, b_s, w_out_b,
           pool_w, pool_scale, w_o, final_norm_g):
    depth = w_in.shape[0]
    batch, seq, d = x_prompt.shape
    dec_batch, steps, _ = x_sample.shape
    d_gmlp = ln_g.shape[-1]
    head_dim = d_gmlp // G_HEADS
    assert seq % TOKEN_TILE == 0 and TOKEN_TILE % CHUNK == 0 and TOKEN_TILE % steps == 0
    assert dec_batch % (TOKEN_TILE // steps) == 0 and steps <= CHUNK and 2 * head_dim == LANES

    c_all = jnp.concatenate([c_sample, c_prompt], axis=0)
    mod_all = _ada_call(c_all, w_ada, b_ada)
    mod_p = mod_all[:, :, dec_batch:, :].transpose(0, 2, 1, 3).reshape(depth, batch, N_SUB, 3, d)

    w1gu, w1dn = _chunk_ffn_weights(w1_gu, w1_dn)
    w2gu, w2dn = _chunk_ffn_weights(w2_gu, w2_dn)
    win = w_in.astype(BF16)
    woa = w_out_a.astype(BF16)
    wob = w_out_b.astype(BF16)
    wo = w_o.astype(BF16)
    poolw = pool_w.astype(BF16)
    ng = norm_g.reshape(depth, N_SUB, 1, d)
    lng = ln_g.reshape(depth, 1, d_gmlp)
    pscale = pool_scale.reshape(depth, 1, d)
    fg = final_norm_g.reshape(1, d)
    w_tril = w_s * jnp.tril(jnp.ones((CHUNK, CHUNK), w_s.dtype))
    wpair = w_tril.reshape(depth, G_HEADS // 2, 2, CHUNK, CHUNK).transpose(0, 1, 3, 2, 4)
    wpair = wpair.reshape(depth, G_HEADS // 2, CHUNK, 2 * CHUNK).astype(BF16)
    sgb = jnp.repeat(b_s.transpose(0, 2, 1), head_dim, axis=-1)
    sgc = jnp.repeat(w_tril[:, :, :steps, :steps].transpose(0, 2, 3, 1), head_dim, axis=-1)
    sgc = sgc.reshape(depth, steps * steps, d_gmlp)

    xp = x_prompt
    xs = x_sample.transpose(1, 0, 2)
    hconv = state_conv.transpose(0, 2, 1, 3)
    hpool = state_pool.transpose(0, 2, 1, 3)

    conv_p, conv_s, pool_p, pool_s, v_s = [], [], [], [], []
    for l in range(depth):
        last = l == depth - 1
        xp = _ffn_call(xp, mod_p, ng, w1gu, w1dn, fg, l=l, sub=0, sample=False, final_norm=False)
        xs = _ffn_call(xs, mod_all, ng, w1gu, w1dn, fg, l=l, sub=0, sample=True, final_norm=False)
        xp, ctail, ptail = _mixer_prompt_call(xp, mod_p, ng, win, conv_w, woa, lng, wpair, sgb,
                                              wob, poolw, pscale, wo, l=l)
        xs, ncs, nps, vs = _mixer_sample_call(xs, mod_all, ng, win, conv_w, woa, lng, sgc, sgb,
                                              wob, poolw, pscale, wo, hconv, hpool, l=l)
        xp = _ffn_call(xp, mod_p, ng, w2gu, w2dn, fg, l=l, sub=2, sample=False, final_norm=last)
        xs = _ffn_call(xs, mod_all, ng, w2gu, w2dn, fg, l=l, sub=2, sample=True, final_norm=last)
        conv_p.append(ctail[:, SUBLANES - (CONV_W - 1):])
        pool_p.append(ptail[:, 1:])
        conv_s.append(ncs.transpose(1, 0, 2))
        pool_s.append(nps.transpose(1, 0, 2))
        v_s.append(vs.transpose(1, 0, 2))

    return (xp, xs.transpose(1, 0, 2), jnp.stack(conv_p), jnp.stack(conv_s), jnp.stack(pool_p),
            jnp.stack(pool_s), jnp.stack(v_s))
```

```python
import functools
import math

import jax
import jax.numpy as jnp
from jax import lax
from jax.experimental import pallas as pl
from jax.experimental.pallas import tpu as pltpu

F32 = jnp.float32
BF16 = jnp.bfloat16

EPS = 1e-6
N_SUB = 3
CONV_W = 3
G_HEADS = 8
CHUNK = 128
POOL_WINDOWS = (2, 4, 8, 16)
MAX_WIN = max(POOL_WINDOWS)
PAST_LEN = 16384

LANES = 128
SUBLANES = 8
MXU_DIM = 256
TOKEN_TILE = 512
FF_CHUNKS = 2
VMEM_LIMIT = 56 * 1024 * 1024


def _flat(a):
    return a.reshape(-1, a.shape[-1])


def _rms_mod(x, g, shift, scale):
    ms = jnp.mean(x * x, axis=-1, keepdims=True)
    y = x * lax.rsqrt(ms + EPS) * g
    return y * (1.0 + scale) + shift


def _dot(a, b):
    return jnp.dot(a, b, preferred_element_type=F32)


def _gelu(x):
    return 0.5 * x * (1.0 + lax.erf(x * (1.0 / math.sqrt(2.0))))


def _ada_kernel(c_ref, w_ref, b_ref, o_ref):
    c = c_ref[...]
    sc = (c * jax.nn.sigmoid(c)).astype(BF16)
    o_ref[...] = _dot(sc, w_ref[...].astype(BF16)) + b_ref[...]


def _ada_call(c_all, w_ada, b_ada):
    depth, d, cols = w_ada.shape
    n_mod = cols // d
    rows = c_all.shape[0]
    return pl.pallas_call(
        _ada_kernel,
        grid=(depth, n_mod),
        in_specs=[
            pl.BlockSpec((rows, d), lambda l, j: (0, 0)),
            pl.BlockSpec((None, d, d), lambda l, j: (l, 0, j)),
            pl.BlockSpec((None, None, 1, d), lambda l, j: (l, j, 0, 0)),
        ],
        out_specs=pl.BlockSpec((None, None, rows, d), lambda l, j: (l, j, 0, 0)),
        out_shape=jax.ShapeDtypeStruct((depth, n_mod, rows, d), F32),
        compiler_params=pltpu.CompilerParams(
            dimension_semantics=("arbitrary", "arbitrary"), vmem_limit_bytes=VMEM_LIMIT),
        name="ada_mod",
    )(c_all, w_ada, b_ada.reshape(depth, n_mod, 1, d))


def _ff_bounds(d_ff):
    tiles = d_ff // MXU_DIM
    per = -(-tiles // FF_CHUNKS)
    return [min(i * per, tiles) * MXU_DIM for i in range(FF_CHUNKS + 1)]


def _ffn_kernel(x_ref, mod_ref, ng_ref, wgu_ref, wdn_ref, fg_ref, o_ref, *, final_norm):
    x = x_ref[...]
    h = _rms_mod(x, ng_ref[...], mod_ref[0:1], mod_ref[1:2])
    hb = _flat(h).astype(BF16)
    d_ff = wdn_ref.shape[0]
    bounds = _ff_bounds(d_ff)
    acc = None
    for c0, c1 in zip(bounds[:-1], bounds[1:]):
        a = _dot(hb, wgu_ref[:, c0:c1])
        b = _dot(hb, wgu_ref[:, d_ff + c0:d_ff + c1])
        g = (a * jax.nn.sigmoid(a)) * b
        part = _dot(g.astype(BF16), wdn_ref[c0:c1, :])
        acc = part if acc is None else acc + part
    y = x + (0.5 * mod_ref[2:3]) * acc.reshape(x.shape)
    if final_norm:
        ms = jnp.mean(y * y, axis=-1, keepdims=True)
        y = y * lax.rsqrt(ms + EPS) * fg_ref[...]
    o_ref[...] = y


def _const_spec(shape, index_map):
    return pl.BlockSpec(shape, index_map, pipeline_mode=pl.Buffered(1))


def _ffn_call(x, mod, ng, wgu, wdn, fg, *, l, sub, sample, final_norm):
    d = x.shape[-1]
    two_ff = wgu.shape[-1]
    if sample:
        steps, batch, _ = x.shape
        bt = TOKEN_TILE // steps
        grid = (batch // bt,)
        x_spec = pl.BlockSpec((steps, bt, d), lambda i: (0, i, 0))
        mod_spec = pl.BlockSpec((None, N_SUB, bt, d), lambda i: (l, sub, i, 0))
        c2 = lambda i: (l, sub, 0, 0)
        c3 = lambda i: (l, 0, 0)
        c0 = lambda i: (0, 0)
    else:
        batch, seq, _ = x.shape
        grid = (batch, seq // TOKEN_TILE)
        x_spec = pl.BlockSpec((None, TOKEN_TILE, d), lambda b, t: (b, t, 0))
        mod_spec = pl.BlockSpec((None, None, None, N_SUB, d), lambda b, t: (l, b, sub, 0, 0))
        c2 = lambda b, t: (l, sub, 0, 0)
        c3 = lambda b, t: (l, 0, 0)
        c0 = lambda b, t: (0, 0)
    return pl.pallas_call(
        functools.partial(_ffn_kernel, final_norm=final_norm),
        grid=grid,
        in_specs=[
            x_spec,
            mod_spec,
            pl.BlockSpec((None, None, 1, d), c2),
            _const_spec((None, d, two_ff), c3),
            _const_spec((None, two_ff // 2, d), c3),
            pl.BlockSpec((1, d), c0),
        ],
        out_specs=x_spec,
        out_shape=jax.ShapeDtypeStruct(x.shape, F32),
        compiler_params=pltpu.CompilerParams(
            dimension_semantics=("arbitrary",) * len(grid), vmem_limit_bytes=VMEM_LIMIT),
        name=("ffn_sample" if sample else "ffn_prompt"),
    )(x, mod, ng, wgu, wdn, fg)


def _mixer_core(x, mod_ref, ng_ref, win_ref, woa_ref, lng_ref, wob_ref, poolw_ref, pscale_ref,
                wo_ref, conv_fn, gate_fn, pool_fn):
    d_conv = woa_ref.shape[0]
    d_gmlp = wob_ref.shape[0]
    d_pool = poolw_ref.shape[0] * poolw_ref.shape[1]
    d_model = wo_ref.shape[0]
    h = _rms_mod(x, ng_ref[...], mod_ref[0:1], mod_ref[1:2])
    hb = _flat(h).astype(BF16)

    win = lambda lo, n: win_ref[:, lo:lo + n]
    o_b = 3 * d_conv
    o_c = o_b + 2 * d_gmlp
    o_g = o_c + d_pool

    pa = _dot(hb, win(0, 3 * d_conv))
    xa = pa[:, :d_conv]
    bg = pa[:, d_conv:2 * d_conv]
    cg = pa[:, 2 * d_conv:]
    conv = conv_fn(cg * xa)
    a_in = (bg * conv).astype(BF16)

    puv = _dot(hb, win(o_b, 2 * d_gmlp))
    y_a = _dot(a_in, woa_ref[...])
    u = _gelu(puv[:, :d_gmlp])
    v = _gelu(puv[:, d_gmlp:])
    mu = jnp.mean(v, axis=-1, keepdims=True)
    vc = v - mu
    var = jnp.mean(vc * vc, axis=-1, keepdims=True)
    v = vc * lax.rsqrt(var + EPS) * lng_ref[...]

    p = _dot(hb, win(o_c, d_pool))
    merged = jax.nn.sigmoid(_dot(hb, win(o_g, d_model))) * y_a
    sg = gate_fn(v)
    gate_b = _dot(hb, win(o_g + d_model, d_model))

    diff = (pool_fn(p) - p).astype(BF16)
    gdim = poolw_ref.shape[1]
    y_b = _dot((u * sg).astype(BF16), wob_ref[...])
    y_c = jnp.concatenate(
        [_dot(diff[:, g * gdim:(g + 1) * gdim], poolw_ref[g]) for g in range(poolw_ref.shape[0])],
        axis=1) * pscale_ref[...]
    gate_c = _dot(hb, win(o_g + 2 * d_model, d_model))
    merged = merged + jax.nn.sigmoid(gate_b) * y_b + jax.nn.sigmoid(gate_c) * y_c
    m = _dot(merged.astype(BF16), wo_ref[...])
    return x + mod_ref[2:3] * m.reshape(x.shape)


def _mixer_prompt_kernel(x_ref, mod_ref, ng_ref, win_ref, cw_ref, woa_ref, lng_ref, wpair_ref,
                         sgb_ref, wob_ref, poolw_ref, pscale_ref, wo_ref,
                         o_ref, ctail_ref, ptail_ref, zc_ref, pc_ref):
    t = pl.program_id(1)
    tm = x_ref.shape[0]

    @pl.when(t == 0)
    def _():
        zc_ref[...] = jnp.zeros_like(zc_ref)
        pc_ref[...] = jnp.zeros_like(pc_ref)

    def conv_fn(z):
        hist = zc_ref.shape[0]
        ext = jnp.concatenate([zc_ref[...], z], axis=0)
        z1 = pltpu.roll(ext, 1, 0)[hist:]
        z2 = pltpu.roll(ext, 2, 0)[hist:]
        tail = z[tm - hist:]
        zc_ref[...] = tail
        ctail_ref[...] = tail
        return z2 * cw_ref[0:1] + z1 * cw_ref[1:2] + z * cw_ref[2:3]

    def gate_fn(v):
        lane = lax.broadcasted_iota(jnp.int32, (CHUNK, LANES), 1)
        low = lane < (LANES // 2)
        n_blk = v.shape[1] // LANES
        rows = []
        for c in range(tm // CHUNK):
            vc = v[c * CHUNK:(c + 1) * CHUNK]
            outs = []
            for j in range(n_blk):
                blk = vc[:, j * LANES:(j + 1) * LANES]
                rhs = jnp.concatenate(
                    [jnp.where(low, blk, 0.0), jnp.where(low, 0.0, blk)], axis=0).astype(BF16)
                outs.append(_dot(wpair_ref[j], rhs))
            rows.append(jnp.concatenate(outs, axis=1) + sgb_ref[...])
        return jnp.concatenate(rows, axis=0)

    def pool_fn(p):
        hist = pc_ref.shape[0]
        ext = jnp.concatenate([pc_ref[...], p], axis=0)
        pos = t * tm + lax.broadcasted_iota(jnp.int32, (tm, 1), 0)
        outs = []
        for gi, w in enumerate(POOL_WINDOWS):
            s = ext[:, gi * LANES:(gi + 1) * LANES]
            k = 1
            while k < w:
                s = s + pltpu.roll(s, k, 0)
                k *= 2
            inv = 1.0 / jnp.minimum(pos + 1, w).astype(F32)
            outs.append(s[hist:] * inv)
        tail = p[tm - hist:]
        pc_ref[...] = tail
        ptail_ref[...] = tail
        return jnp.concatenate(outs, axis=1)

    o_ref[...] = _mixer_core(x_ref[...], mod_ref, ng_ref, win_ref, woa_ref, lng_ref, wob_ref,
                             poolw_ref, pscale_ref, wo_ref, conv_fn, gate_fn, pool_fn)


def _mixer_sample_kernel(x_ref, mod_ref, ng_ref, win_ref, cw_ref, woa_ref, lng_ref, sgc_ref,
                         sgb_ref, wob_ref, poolw_ref, pscale_ref, wo_ref, hc_ref, hp_ref,
                         o_ref, nconv_ref, npool_ref, vopen_ref):
    steps, bt, _ = x_ref.shape

    def conv_fn(z):
        z3 = z.reshape(steps, bt, z.shape[-1])
        ext = jnp.concatenate([hc_ref[...], z3], axis=0)
        nconv_ref[...] = ext[steps:]
        conv = sum(ext[k:k + steps] * cw_ref[k:k + 1] for k in range(CONV_W))
        return _flat(conv)

    def gate_fn(v):
        v3 = v.reshape(steps, bt, v.shape[-1])
        vopen_ref[...] = v3
        outs = []
        for tt in range(steps):
            acc = sgb_ref[tt:tt + 1] + sgc_ref[tt * steps:tt * steps + 1] * v3[0]
            for s in range(1, tt + 1):
                acc = acc + sgc_ref[tt * steps + s:tt * steps + s + 1] * v3[s]
            outs.append(acc)
        return _flat(jnp.stack(outs, axis=0))

    def pool_fn(p):
        p3 = p.reshape(steps, bt, p.shape[-1])
        ext = jnp.concatenate([hp_ref[...], p3], axis=0)
        hist = hp_ref.shape[0]
        npool_ref[...] = ext[steps:]
        outs = []
        for gi, w in enumerate(POOL_WINDOWS):
            s = ext[:, :, gi * LANES:(gi + 1) * LANES]
            k = 1
            while k < w:
                s = s[k:] + s[:-k]
                k *= 2
            first = hist - (w - 1)
            outs.append(jnp.stack(
                [s[first + tt] * (1.0 / min(PAST_LEN + tt + 1, w)) for tt in range(steps)], axis=0))
        return _flat(jnp.concatenate(outs, axis=-1))

    o_ref[...] = _mixer_core(x_ref[...], mod_ref, ng_ref, win_ref, woa_ref, lng_ref, wob_ref,
                             poolw_ref, pscale_ref, wo_ref, conv_fn, gate_fn, pool_fn)


def _mixer_prompt_call(x, mod, ng, win, cw, woa, lng, wpair, sgb, wob, poolw, pscale, wo, *, l):
    batch, seq, d = x.shape
    d_conv = cw.shape[-1]
    d_pool = poolw.shape[2] * poolw.shape[1]
    sub = 1
    tm = TOKEN_TILE
    cl = lambda *tail: (lambda b, t: (l,) + tail)
    full = lambda a: _const_spec((None,) + a.shape[1:], cl(*([0] * (a.ndim - 1))))
    return pl.pallas_call(
        _mixer_prompt_kernel,
        grid=(batch, seq // tm),
        in_specs=[
            pl.BlockSpec((None, tm, d), lambda b, t: (b, t, 0)),
            pl.BlockSpec((None, None, None, N_SUB, d), lambda b, t: (l, b, sub, 0, 0)),
            pl.BlockSpec((None, None, 1, d), lambda b, t: (l, sub, 0, 0)),
            full(win), full(cw), full(woa), full(lng), full(wpair), full(sgb), full(wob),
            full(poolw), full(pscale), full(wo),
        ],
        out_specs=[
            pl.BlockSpec((None, tm, d), lambda b, t: (b, t, 0)),
            pl.BlockSpec((None, SUBLANES, d_conv), lambda b, t: (b, 0, 0)),
            pl.BlockSpec((None, MAX_WIN, d_pool), lambda b, t: (b, 0, 0)),
        ],
        out_shape=[
            jax.ShapeDtypeStruct(x.shape, F32),
            jax.ShapeDtypeStruct((batch, SUBLANES, d_conv), F32),
            jax.ShapeDtypeStruct((batch, MAX_WIN, d_pool), F32),
        ],
        scratch_shapes=[pltpu.VMEM((SUBLANES, d_conv), F32), pltpu.VMEM((MAX_WIN, d_pool), F32)],
        compiler_params=pltpu.CompilerParams(
            dimension_semantics=("arbitrary", "arbitrary"), vmem_limit_bytes=VMEM_LIMIT),
        name="mixer_prompt",
    )(x, mod, ng, win, cw, woa, lng, wpair, sgb, wob, poolw, pscale, wo)


def _mixer_sample_call(x, mod, ng, win, cw, woa, lng, sgc, sgb, wob, poolw, pscale, wo,
                       hconv, hpool, *, l):
    steps, batch, d = x.shape
    d_conv = cw.shape[-1]
    d_gmlp = wob.shape[1]
    d_pool = poolw.shape[2] * poolw.shape[1]
    sub = 1
    bt = TOKEN_TILE // steps
    cl = lambda *tail: (lambda i: (l,) + tail)
    full = lambda a: _const_spec((None,) + a.shape[1:], cl(*([0] * (a.ndim - 1))))
    n_hc, n_hp = hconv.shape[1], hpool.shape[1]
    return pl.pallas_call(
        _mixer_sample_kernel,
        grid=(batch // bt,),
        in_specs=[
            pl.BlockSpec((steps, bt, d), lambda i: (0, i, 0)),
            pl.BlockSpec((None, N_SUB, bt, d), lambda i: (l, sub, i, 0)),
            pl.BlockSpec((None, None, 1, d), lambda i: (l, sub, 0, 0)),
            full(win), full(cw), full(woa), full(lng), full(sgc), full(sgb), full(wob),
            full(poolw), full(pscale), full(wo),
            pl.BlockSpec((None, n_hc, bt, d_conv), lambda i: (l, 0, i, 0)),
            pl.BlockSpec((None, n_hp, bt, d_pool), lambda i: (l, 0, i, 0)),
        ],
        out_specs=[
            pl.BlockSpec((steps, bt, d), lambda i: (0, i, 0)),
            pl.BlockSpec((n_hc, bt, d_conv), lambda i: (0, i, 0)),
            pl.BlockSpec((n_hp, bt, d_pool), lambda i: (0, i, 0)),
            pl.BlockSpec((steps, bt, d_gmlp), lambda i: (0, i, 0)),
        ],
        out_shape=[
            jax.ShapeDtypeStruct(x.shape, F32),
            jax.ShapeDtypeStruct((n_hc, batch, d_conv), F32),
            jax.ShapeDtypeStruct((n_hp, batch, d_pool), F32),
            jax.ShapeDtypeStruct((steps, batch, d_gmlp), F32),
        ],
        compiler_params=pltpu.CompilerParams(
            dimension_semantics=("arbitrary",), vmem_limit_bytes=VMEM_LIMIT),
        name="mixer_sample",
    )(x, mod, ng, win, cw, woa, lng, sgc, sgb, wob, poolw, pscale, wo, hconv, hpool)


def kernel(x_prompt, x_sample, state_conv, state_pool, c_prompt, c_sample, norm_g, w_ada, b_ada,
           w1_gu, w1_dn, w2_gu, w2_dn, w_in, conv_w, w_out_a, ln_g, w_s, b_s, w_out_b,
           pool_w, pool_scale, w_o, final_norm_g):
    depth = w_in.shape[0]
    batch, seq, d = x_prompt.shape
    dec_batch, steps, _ = x_sample.shape
    d_gmlp = ln_g.shape[-1]
    head_dim = d_gmlp // G_HEADS
    assert seq % TOKEN_TILE == 0 and TOKEN_TILE % CHUNK == 0 and TOKEN_TILE % steps == 0
    assert dec_batch % (TOKEN_TILE // steps) == 0 and steps <= CHUNK and 2 * head_dim == LANES

    c_all = jnp.concatenate([c_sample, c_prompt], axis=0)
    mod_all = _ada_call(c_all, w_ada, b_ada)
    mod_p = mod_all[:, :, dec_batch:, :].transpose(0, 2, 1, 3).reshape(depth, batch, N_SUB, 3, d)

    w1gu, w1dn = w1_gu.astype(BF16), w1_dn.astype(BF16)
    w2gu, w2dn = w2_gu.astype(BF16), w2_dn.astype(BF16)
    win = w_in.astype(BF16)
    woa = w_out_a.astype(BF16)
    wob = w_out_b.astype(BF16)
    wo = w_o.astype(BF16)
    poolw = pool_w.astype(BF16)
    ng = norm_g.reshape(depth, N_SUB, 1, d)
    lng = ln_g.reshape(depth, 1, d_gmlp)
    pscale = pool_scale.reshape(depth, 1, d)
    fg = final_norm_g.reshape(1, d)
    w_tril = w_s * jnp.tril(jnp.ones((CHUNK, CHUNK), w_s.dtype))
    wpair = w_tril.reshape(depth, G_HEADS // 2, 2, CHUNK, CHUNK).transpose(0, 1, 3, 2, 4)
    wpair = wpair.reshape(depth, G_HEADS // 2, CHUNK, 2 * CHUNK).astype(BF16)
    sgb = jnp.repeat(b_s.transpose(0, 2, 1), head_dim, axis=-1)
    sgc = jnp.repeat(w_tril[:, :, :steps, :steps].transpose(0, 2, 3, 1), head_dim, axis=-1)
    sgc = sgc.reshape(depth, steps * steps, d_gmlp)

    xp = x_prompt
    xs = x_sample.transpose(1, 0, 2)
    hconv = state_conv.transpose(0, 2, 1, 3)
    hpool = state_pool.transpose(0, 2, 1, 3)

    conv_p, conv_s, pool_p, pool_s, v_s = [], [], [], [], []
    for l in range(depth):
        last = l == depth - 1
        xp = _ffn_call(xp, mod_p, ng, w1gu, w1dn, fg, l=l, sub=0, sample=False, final_norm=False)
        xs = _ffn_call(xs, mod_all, ng, w1gu, w1dn, fg, l=l, sub=0, sample=True, final_norm=False)
        xp, ctail, ptail = _mixer_prompt_call(xp, mod_p, ng, win, conv_w, woa, lng, wpair, sgb,
                                              wob, poolw, pscale, wo, l=l)
        xs, ncs, nps, vs = _mixer_sample_call(xs, mod_all, ng, win, conv_w, woa, lng, sgc, sgb,
                                              wob, poolw, pscale, wo, hconv, hpool, l=l)
        xp = _ffn_call(xp, mod_p, ng, w2gu, w2dn, fg, l=l, sub=2, sample=False, final_norm=last)
        xs = _ffn_call(xs, mod_all, ng, w2gu, w2dn, fg, l=l, sub=2, sample=True, final_norm=last)
        conv_p.append(ctail[:, SUBLANES - (CONV_W - 1):])
        pool_p.append(ptail[:, 1:])
        conv_s.append(ncs.transpose(1, 0, 2))
        pool_s.append(nps.transpose(1, 0, 2))
        v_s.append(vs.transpose(1, 0, 2))

    return (xp, xs.transpose(1, 0, 2), jnp.stack(conv_p), jnp.stack(conv_s), jnp.stack(pool_p),
            jnp.stack(pool_s), jnp.stack(v_s))
```

```python
import functools
import math

import jax
import jax.numpy as jnp
from jax import lax
from jax.experimental import pallas as pl
from jax.experimental.pallas import tpu as pltpu

F32 = jnp.float32
BF16 = jnp.bfloat16

EPS = 1e-6
N_SUB = 3
CONV_W = 3
G_HEADS = 8
CHUNK = 128
POOL_WINDOWS = (2, 4, 8, 16)
MAX_WIN = max(POOL_WINDOWS)
PAST_LEN = 16384

LANES = 128
SUBLANES = 8
MXU_DIM = 256
TOKEN_TILE = 512
FF_CHUNKS = 2
VMEM_LIMIT = 56 * 1024 * 1024


def _flat(a):
    return a.reshape(-1, a.shape[-1])


def _rms_mod(x, g, shift, scale):
    ms = jnp.mean(x * x, axis=-1, keepdims=True)
    y = x * lax.rsqrt(ms + EPS) * g
    return y * (1.0 + scale) + shift


_dot = functools.partial(jnp.dot, preferred_element_type=F32)


def _gelu(x):
    return 0.5 * x * (1.0 + lax.erf(x * (1.0 / math.sqrt(2.0))))


def _ada_kernel(c_ref, w_ref, b_ref, o_ref):
    c = c_ref[...]
    sc = (c * jax.nn.sigmoid(c)).astype(BF16)
    o_ref[...] = _dot(sc, w_ref[...].astype(BF16)) + b_ref[...]


def _ada_call(c_all, w_ada, b_ada):
    depth, d, cols = w_ada.shape
    n_mod = cols // d
    rows = c_all.shape[0]
    return pl.pallas_call(
        _ada_kernel,
        grid=(depth, n_mod),
        in_specs=[
            pl.BlockSpec((rows, d), lambda l, j: (0, 0)),
            pl.BlockSpec((None, d, d), lambda l, j: (l, 0, j)),
            pl.BlockSpec((None, None, 1, d), lambda l, j: (l, j, 0, 0)),
        ],
        out_specs=pl.BlockSpec((None, None, rows, d), lambda l, j: (l, j, 0, 0)),
        out_shape=jax.ShapeDtypeStruct((depth, n_mod, rows, d), F32),
        compiler_params=pltpu.CompilerParams(
            dimension_semantics=("arbitrary", "arbitrary"), vmem_limit_bytes=VMEM_LIMIT),
        name="ada_mod",
    )(c_all, w_ada, b_ada.reshape(depth, n_mod, 1, d))


def _ff_bounds(d_ff):
    tiles = d_ff // MXU_DIM
    per = -(-tiles // FF_CHUNKS)
    return [min(i * per, tiles) * MXU_DIM for i in range(FF_CHUNKS + 1)]


def _ffn_kernel(x_ref, mod_ref, ng_ref, wgu_ref, wdn_ref, fg_ref, o_ref, *, final_norm):
    x = x_ref[...]
    h = _rms_mod(x, ng_ref[...], mod_ref[0:1], mod_ref[1:2])
    hb = _flat(h).astype(BF16)
    d_ff = wdn_ref.shape[0]
    bounds = _ff_bounds(d_ff)
    acc = None
    for c0, c1 in zip(bounds[:-1], bounds[1:]):
        a = _dot(hb, wgu_ref[:, c0:c1])
        b = _dot(hb, wgu_ref[:, d_ff + c0:d_ff + c1])
        g = (a * jax.nn.sigmoid(a)) * b
        part = _dot(g.astype(BF16), wdn_ref[c0:c1, :])
        acc = part if acc is None else acc + part
    y = x + (0.5 * mod_ref[2:3]) * acc.reshape(x.shape)
    if final_norm:
        ms = jnp.mean(y * y, axis=-1, keepdims=True)
        y = y * lax.rsqrt(ms + EPS) * fg_ref[...]
    o_ref[...] = y


def _const_spec(shape, index_map):
    return pl.BlockSpec(shape, index_map, pipeline_mode=pl.Buffered(1))


def _ffn_call(x, mod, ng, wgu, wdn, fg, *, l, sub, sample, final_norm):
    d = x.shape[-1]
    two_ff = wgu.shape[-1]
    if sample:
        steps, batch, _ = x.shape
        bt = TOKEN_TILE // steps
        grid = (batch // bt,)
        x_spec = pl.BlockSpec((steps, bt, d), lambda i: (0, i, 0))
        mod_spec = pl.BlockSpec((None, N_SUB, bt, d), lambda i: (l, sub, i, 0))
        c2 = lambda i: (l, sub, 0, 0)
        c3 = lambda i: (l, 0, 0)
        c0 = lambda i: (0, 0)
    else:
        batch, seq, _ = x.shape
        grid = (batch, seq // TOKEN_TILE)
        x_spec = pl.BlockSpec((None, TOKEN_TILE, d), lambda b, t: (b, t, 0))
        mod_spec = pl.BlockSpec((None, None, None, N_SUB, d), lambda b, t: (l, b, sub, 0, 0))
        c2 = lambda b, t: (l, sub, 0, 0)
        c3 = lambda b, t: (l, 0, 0)
        c0 = lambda b, t: (0, 0)
    return pl.pallas_call(
        functools.partial(_ffn_kernel, final_norm=final_norm),
        grid=grid,
        in_specs=[
            x_spec,
            mod_spec,
            pl.BlockSpec((None, None, 1, d), c2),
            _const_spec((None, d, two_ff), c3),
            _const_spec((None, two_ff // 2, d), c3),
            pl.BlockSpec((1, d), c0),
        ],
        out_specs=x_spec,
        out_shape=jax.ShapeDtypeStruct(x.shape, F32),
        compiler_params=pltpu.CompilerParams(
            dimension_semantics=("arbitrary",) * len(grid), vmem_limit_bytes=VMEM_LIMIT),
        name=("ffn_sample" if sample else "ffn_prompt"),
    )(x, mod, ng, wgu, wdn, fg)


def _mixer_core(x, mod_ref, ng_ref, win_ref, woa_ref, lng_ref, wob_ref, poolw_ref, pscale_ref,
                wo_ref, conv_fn, gate_fn, pool_fn):
    d_conv = woa_ref.shape[0]
    d_gmlp = wob_ref.shape[0]
    d_pool = poolw_ref.shape[0] * poolw_ref.shape[1]
    d_model = wo_ref.shape[0]
    h = _rms_mod(x, ng_ref[...], mod_ref[0:1], mod_ref[1:2])
    hb = _flat(h).astype(BF16)

    win = lambda lo, n: win_ref[:, lo:lo + n]
    o_b = 3 * d_conv
    o_c = o_b + 2 * d_gmlp
    o_g = o_c + d_pool

    pa = _dot(hb, win(0, 3 * d_conv))
    xa = pa[:, :d_conv]
    bg = pa[:, d_conv:2 * d_conv]
    cg = pa[:, 2 * d_conv:]
    conv = conv_fn(cg * xa)
    a_in = (bg * conv).astype(BF16)

    puv = _dot(hb, win(o_b, 2 * d_gmlp))
    y_a = _dot(a_in, woa_ref[...])
    u = _gelu(puv[:, :d_gmlp])
    v = _gelu(puv[:, d_gmlp:])
    mu = jnp.mean(v, axis=-1, keepdims=True)
    vc = v - mu
    var = jnp.mean(vc * vc, axis=-1, keepdims=True)
    v = vc * lax.rsqrt(var + EPS) * lng_ref[...]

    p = _dot(hb, win(o_c, d_pool))
    merged = jax.nn.sigmoid(_dot(hb, win(o_g, d_model))) * y_a
    sg = gate_fn(v)
    gate_b = _dot(hb, win(o_g + d_model, d_model))

    diff = (pool_fn(p) - p).astype(BF16)
    gdim = poolw_ref.shape[1]
    y_b = _dot((u * sg).astype(BF16), wob_ref[...])
    y_c = jnp.concatenate(
        [_dot(diff[:, g * gdim:(g + 1) * gdim], poolw_ref[g]) for g in range(poolw_ref.shape[0])],
        axis=1) * pscale_ref[...]
    gate_c = _dot(hb, win(o_g + 2 * d_model, d_model))
    merged = merged + jax.nn.sigmoid(gate_b) * y_b + jax.nn.sigmoid(gate_c) * y_c
    m = _dot(merged.astype(BF16), wo_ref[...])
    return x + mod_ref[2:3] * m.reshape(x.shape)


def _mixer_prompt_kernel(x_ref, mod_ref, ng_ref, win_ref, cw_ref, woa_ref, lng_ref, wpair_ref,
                         sgb_ref, wob_ref, poolw_ref, pscale_ref, wo_ref,
                         o_ref, ctail_ref, ptail_ref, zc_ref, pc_ref, pd_ref):
    t = pl.program_id(1)
    tm = x_ref.shape[0]
    zh = zc_ref.shape[1] - tm
    ph = pc_ref.shape[1] - tm

    @pl.when(t == 0)
    def _():
        zc_ref[:, 0:zh, :] = jnp.zeros((zc_ref.shape[0], zh, LANES), F32)
        pc_ref[:, 0:ph, :] = jnp.zeros((pc_ref.shape[0], ph, LANES), F32)

    def conv_fn(z):
        outs = []
        for g in range(zc_ref.shape[0]):
            ln = slice(g * LANES, (g + 1) * LANES)
            zg = z[:, ln]
            zc_ref[g, zh:, :] = zg
            acc = zg * cw_ref[CONV_W - 1:CONV_W, ln]
            for k in range(1, CONV_W):
                acc = acc + zc_ref[g, zh - k:zh - k + tm, :] * cw_ref[CONV_W - 1 - k:CONV_W - k, ln]
            tail = zg[tm - SUBLANES:]
            zc_ref[g, zh - SUBLANES:zh, :] = tail
            ctail_ref[:, ln] = tail
            outs.append(acc)
        return jnp.concatenate(outs, axis=1)

    def gate_fn(v):
        lane = lax.broadcasted_iota(jnp.int32, (CHUNK, LANES), 1)
        low = lane < (LANES // 2)
        n_blk = v.shape[1] // LANES
        rows = []
        for c in range(tm // CHUNK):
            vc = v[c * CHUNK:(c + 1) * CHUNK]
            outs = []
            for j in range(n_blk):
                blk = vc[:, j * LANES:(j + 1) * LANES]
                rhs = jnp.concatenate(
                    [jnp.where(low, blk, 0.0), jnp.where(low, 0.0, blk)], axis=0).astype(BF16)
                outs.append(_dot(wpair_ref[j], rhs))
            rows.append(jnp.concatenate(outs, axis=1) + sgb_ref[...])
        return jnp.concatenate(rows, axis=0)

    def pool_fn(p):
        pos = t * tm + lax.broadcasted_iota(jnp.int32, (MAX_WIN, 1), 0)
        outs = []
        buf = 0
        for gi, w in enumerate(POOL_WINDOWS):
            ln = slice(gi * LANES, (gi + 1) * LANES)
            pg = p[:, ln]
            pc_ref[gi, ph:, :] = pg
            src = pc_ref.at[gi]
            k = 1
            lo = 0
            while 2 * k < w:
                lo += SUBLANES
                dst = pd_ref.at[buf]
                buf += 1
                dst[lo:, :] = src[lo:, :] + src[lo - k:ph + tm - k, :]
                src = dst
                k *= 2
            s = src[ph:, :] + src[ph - k:ph + tm - k, :]
            head = s[:MAX_WIN] * (1.0 / jnp.minimum(pos + 1, w).astype(F32))
            outs.append(jnp.concatenate([head, s[MAX_WIN:] * (1.0 / w)], axis=0))
            tail = pg[tm - MAX_WIN:]
            pc_ref[gi, ph - MAX_WIN:ph, :] = tail
            ptail_ref[:, ln] = tail
        return jnp.concatenate(outs, axis=1)

    o_ref[...] = _mixer_core(x_ref[...], mod_ref, ng_ref, win_ref, woa_ref, lng_ref, wob_ref,
                             poolw_ref, pscale_ref, wo_ref, conv_fn, gate_fn, pool_fn)


def _mixer_sample_kernel(x_ref, mod_ref, ng_ref, win_ref, cw_ref, woa_ref, lng_ref, sgc_ref,
                         sgb_ref, wob_ref, poolw_ref, pscale_ref, wo_ref, hc_ref, hp_ref,
                         o_ref, nconv_ref, npool_ref, vopen_ref):
    steps, bt, _ = x_ref.shape

    def conv_fn(z):
        z3 = z.reshape(steps, bt, z.shape[-1])
        ext = jnp.concatenate([hc_ref[...], z3], axis=0)
        nconv_ref[...] = ext[steps:]
        conv = sum(ext[k:k + steps] * cw_ref[k:k + 1] for k in range(CONV_W))
        return _flat(conv)

    def gate_fn(v):
        v3 = v.reshape(steps, bt, v.shape[-1])
        vopen_ref[...] = v3
        outs = []
        for tt in range(steps):
            acc = sgb_ref[tt:tt + 1] + sgc_ref[tt * steps:tt * steps + 1] * v3[0]
            for s in range(1, tt + 1):
                acc = acc + sgc_ref[tt * steps + s:tt * steps + s + 1] * v3[s]
            outs.append(acc)
        return _flat(jnp.stack(outs, axis=0))

    def pool_fn(p):
        p3 = p.reshape(steps, bt, p.shape[-1])
        ext = jnp.concatenate([hp_ref[...], p3], axis=0)
        hist = hp_ref.shape[0]
        npool_ref[...] = ext[steps:]
        outs = []
        for gi, w in enumerate(POOL_WINDOWS):
            s = ext[:, :, gi * LANES:(gi + 1) * LANES]
            k = 1
            while k < w:
                s = s[k:] + s[:-k]
                k *= 2
            first = hist - (w - 1)
            outs.append(jnp.stack(
                [s[first + tt] * (1.0 / min(PAST_LEN + tt + 1, w)) for tt in range(steps)], axis=0))
        return _flat(jnp.concatenate(outs, axis=-1))

    o_ref[...] = _mixer_core(x_ref[...], mod_ref, ng_ref, win_ref, woa_ref, lng_ref, wob_ref,
                             poolw_ref, pscale_ref, wo_ref, conv_fn, gate_fn, pool_fn)


def _mixer_prompt_call(x, mod, ng, win, cw, woa, lng, wpair, sgb, wob, poolw, pscale, wo, *, l):
    batch, seq, d = x.shape
    d_conv = cw.shape[-1]
    d_pool = poolw.shape[2] * poolw.shape[1]
    sub = 1
    tm = TOKEN_TILE
    n_pool_tmp = sum(max(w.bit_length() - 2, 0) for w in POOL_WINDOWS)
    cl = lambda *tail: (lambda b, t: (l,) + tail)
    full = lambda a: _const_spec((None,) + a.shape[1:], cl(*([0] * (a.ndim - 1))))
    return pl.pallas_call(
        _mixer_prompt_kernel,
        grid=(batch, seq // tm),
        in_specs=[
            pl.BlockSpec((None, tm, d), lambda b, t: (b, t, 0)),
            pl.BlockSpec((None, None, None, N_SUB, d), lambda b, t: (l, b, sub, 0, 0)),
            pl.BlockSpec((None, None, 1, d), lambda b, t: (l, sub, 0, 0)),
            full(win), full(cw), full(woa), full(lng), full(wpair), full(sgb), full(wob),
            full(poolw), full(pscale), full(wo),
        ],
        out_specs=[
            pl.BlockSpec((None, tm, d), lambda b, t: (b, t, 0)),
            pl.BlockSpec((None, SUBLANES, d_conv), lambda b, t: (b, 0, 0)),
            pl.BlockSpec((None, MAX_WIN, d_pool), lambda b, t: (b, 0, 0)),
        ],
        out_shape=[
            jax.ShapeDtypeStruct(x.shape, F32),
            jax.ShapeDtypeStruct((batch, SUBLANES, d_conv), F32),
            jax.ShapeDtypeStruct((batch, MAX_WIN, d_pool), F32),
        ],
        scratch_shapes=[
            pltpu.VMEM((d_conv // LANES, SUBLANES + tm, LANES), F32),
            pltpu.VMEM((d_pool // LANES, 2 * MAX_WIN + tm, LANES), F32),
            pltpu.VMEM((n_pool_tmp, 2 * MAX_WIN + tm, LANES), F32),
        ],
        compiler_params=pltpu.CompilerParams(
            dimension_semantics=("arbitrary", "arbitrary"), vmem_limit_bytes=VMEM_LIMIT),
        name="mixer_prompt",
    )(x, mod, ng, win, cw, woa, lng, wpair, sgb, wob, poolw, pscale, wo)


def _mixer_sample_call(x, mod, ng, win, cw, woa, lng, sgc, sgb, wob, poolw, pscale, wo,
                       hconv, hpool, *, l):
    steps, batch, d = x.shape
    d_conv = cw.shape[-1]
    d_gmlp = wob.shape[1]
    d_pool = poolw.shape[2] * poolw.shape[1]
    sub = 1
    bt = TOKEN_TILE // steps
    cl = lambda *tail: (lambda i: (l,) + tail)
    full = lambda a: _const_spec((None,) + a.shape[1:], cl(*([0] * (a.ndim - 1))))
    n_hc, n_hp = hconv.shape[1], hpool.shape[1]
    return pl.pallas_call(
        _mixer_sample_kernel,
        grid=(batch // bt,),
        in_specs=[
            pl.BlockSpec((steps, bt, d), lambda i: (0, i, 0)),
            pl.BlockSpec((None, N_SUB, bt, d), lambda i: (l, sub, i, 0)),
            pl.BlockSpec((None, None, 1, d), lambda i: (l, sub, 0, 0)),
            full(win), full(cw), full(woa), full(lng), full(sgc), full(sgb), full(wob),
            full(poolw), full(pscale), full(wo),
            pl.BlockSpec((None, n_hc, bt, d_conv), lambda i: (l, 0, i, 0)),
            pl.BlockSpec((None, n_hp, bt, d_pool), lambda i: (l, 0, i, 0)),
        ],
        out_specs=[
            pl.BlockSpec((steps, bt, d), lambda i: (0, i, 0)),
            pl.BlockSpec((n_hc, bt, d_conv), lambda i: (0, i, 0)),
            pl.BlockSpec((n_hp, bt, d_pool), lambda i: (0, i, 0)),
            pl.BlockSpec((steps, bt, d_gmlp), lambda i: (0, i, 0)),
        ],
        out_shape=[
            jax.ShapeDtypeStruct(x.shape, F32),
            jax.ShapeDtypeStruct((n_hc, batch, d_conv), F32),
            jax.ShapeDtypeStruct((n_hp, batch, d_pool), F32),
            jax.ShapeDtypeStruct((steps, batch, d_gmlp), F32),
        ],
        compiler_params=pltpu.CompilerParams(
            dimension_semantics=("arbitrary",), vmem_limit_bytes=VMEM_LIMIT),
        name="mixer_sample",
    )(x, mod, ng, win, cw, woa, lng, sgc, sgb, wob, poolw, pscale, wo, hconv, hpool)


def kernel(x_prompt, x_sample, state_conv, state_pool, c_prompt, c_sample, norm_g, w_ada, b_ada,
           w1_gu, w1_dn, w2_gu, w2_dn, w_in, conv_w, w_out_a, ln_g, w_s, b_s, w_out_b,
           pool_w, pool_scale, w_o, final_norm_g):
    depth = w_in.shape[0]
    batch, seq, d = x_prompt.shape
    dec_batch, steps, _ = x_sample.shape
    d_gmlp = ln_g.shape[-1]
    head_dim = d_gmlp // G_HEADS
    assert seq % TOKEN_TILE == 0 and TOKEN_TILE % CHUNK == 0 and TOKEN_TILE % steps == 0
    assert dec_batch % (TOKEN_TILE // steps) == 0 and steps <= CHUNK and 2 * head_dim == LANES

    c_all = jnp.concatenate([c_sample, c_prompt], axis=0)
    mod_all = _ada_call(c_all, w_ada, b_ada)
    mod_p = mod_all[:, :, dec_batch:, :].transpose(0, 2, 1, 3).reshape(depth, batch, N_SUB, 3, d)

    w1gu, w1dn = w1_gu.astype(BF16), w1_dn.astype(BF16)
    w2gu, w2dn = w2_gu.astype(BF16), w2_dn.astype(BF16)
    win = w_in.astype(BF16)
    woa = w_out_a.astype(BF16)
    wob = w_out_b.astype(BF16)
    wo = w_o.astype(BF16)
    poolw = pool_w.astype(BF16)
    ng = norm_g.reshape(depth, N_SUB, 1, d)
    lng = ln_g.reshape(depth, 1, d_gmlp)
    pscale = pool_scale.reshape(depth, 1, d)
    fg = final_norm_g.reshape(1, d)
    w_tril = w_s * jnp.tril(jnp.ones((CHUNK, CHUNK), w_s.dtype))
    wpair = w_tril.reshape(depth, G_HEADS // 2, 2, CHUNK, CHUNK).transpose(0, 1, 3, 2, 4)
    wpair = wpair.reshape(depth, G_HEADS // 2, CHUNK, 2 * CHUNK).astype(BF16)
    sgb = jnp.repeat(b_s.transpose(0, 2, 1), head_dim, axis=-1)
    sgc = jnp.repeat(w_tril[:, :, :steps, :steps].transpose(0, 2, 3, 1), head_dim, axis=-1)
    sgc = sgc.reshape(depth, steps * steps, d_gmlp)

    xp = x_prompt
    xs = x_sample.transpose(1, 0, 2)
    hconv = state_conv.transpose(0, 2, 1, 3)
    hpool = state_pool.transpose(0, 2, 1, 3)

    conv_p, conv_s, pool_p, pool_s, v_s = [], [], [], [], []
    for l in range(depth):
        last = l == depth - 1
        xp = _ffn_call(xp, mod_p, ng, w1gu, w1dn, fg, l=l, sub=0, sample=False, final_norm=False)
        xs = _ffn_call(xs, mod_all, ng, w1gu, w1dn, fg, l=l, sub=0, sample=True, final_norm=False)
        xp, ctail, ptail = _mixer_prompt_call(xp, mod_p, ng, win, conv_w, woa, lng, wpair, sgb,
                                              wob, poolw, pscale, wo, l=l)
        xs, ncs, nps, vs = _mixer_sample_call(xs, mod_all, ng, win, conv_w, woa, lng, sgc, sgb,
                                              wob, poolw, pscale, wo, hconv, hpool, l=l)
        xp = _ffn_call(xp, mod_p, ng, w2gu, w2dn, fg, l=l, sub=2, sample=False, final_norm=last)
        xs = _ffn_call(xs, mod_all, ng, w2gu, w2dn, fg, l=l, sub=2, sample=True, final_norm=last)
        conv_p.append(ctail[:, SUBLANES - (CONV_W - 1):])
        pool_p.append(ptail[:, 1:])
        conv_s.append(ncs.transpose(1, 0, 2))
        pool_s.append(nps.transpose(1, 0, 2))
        v_s.append(vs.transpose(1, 0, 2))

    return (xp, xs.transpose(1, 0, 2), jnp.stack(conv_p), jnp.stack(conv_s), jnp.stack(pool_p),
            jnp.stack(pool_s), jnp.stack(v_s))
```

```python
import functools
import math

import jax
import jax.numpy as jnp
from jax import lax
from jax.experimental import pallas as pl
from jax.experimental.pallas import tpu as pltpu

F32 = jnp.float32
BF16 = jnp.bfloat16

EPS = 1e-6
N_SUB = 3
CONV_W = 3
G_HEADS = 8
CHUNK = 128
POOL_WINDOWS = (2, 4, 8, 16)
MAX_WIN = max(POOL_WINDOWS)
PAST_LEN = 16384

LANES = 128
SUBLANES = 8
MXU_DIM = 256
TOKEN_TILE = 512
SUB_TILES = 2
STEP_TOKENS = TOKEN_TILE * SUB_TILES
FF_CHUNKS = 2
VMEM_LIMIT = 56 * 1024 * 1024


def _flat(a):
    return a.reshape(-1, a.shape[-1])


def _rms_mod(x, g, shift, scale):
    ms = jnp.mean(x * x, axis=-1, keepdims=True)
    y = x * lax.rsqrt(ms + EPS) * g
    return y * (1.0 + scale) + shift


_dot = functools.partial(jnp.dot, preferred_element_type=F32)


def _gelu(x):
    return 0.5 * x * (1.0 + lax.erf(x * (1.0 / math.sqrt(2.0))))


def _ada_kernel(c_ref, w_ref, b_ref, o_ref):
    c = c_ref[...]
    sc = (c * jax.nn.sigmoid(c)).astype(BF16)
    o_ref[...] = _dot(sc, w_ref[...].astype(BF16)) + b_ref[...]


def _ada_call(c_all, w_ada, b_ada):
    depth, d, cols = w_ada.shape
    n_mod = cols // d
    rows = c_all.shape[0]
    return pl.pallas_call(
        _ada_kernel,
        grid=(depth, n_mod),
        in_specs=[
            pl.BlockSpec((rows, d), lambda l, j: (0, 0)),
            pl.BlockSpec((None, d, d), lambda l, j: (l, 0, j)),
            pl.BlockSpec((None, None, 1, d), lambda l, j: (l, j, 0, 0)),
        ],
        out_specs=pl.BlockSpec((None, None, rows, d), lambda l, j: (l, j, 0, 0)),
        out_shape=jax.ShapeDtypeStruct((depth, n_mod, rows, d), F32),
        compiler_params=pltpu.CompilerParams(
            dimension_semantics=("arbitrary", "arbitrary"), vmem_limit_bytes=VMEM_LIMIT),
        name="ada_mod",
    )(c_all, w_ada, b_ada.reshape(depth, n_mod, 1, d))


def _ff_bounds(d_ff):
    tiles = d_ff // MXU_DIM
    per = -(-tiles // FF_CHUNKS)
    return [min(i * per, tiles) * MXU_DIM for i in range(FF_CHUNKS + 1)]


def _rows(ref, i, n):
    start = pl.multiple_of(i * n, n)
    return ref.at[(slice(None),) * (len(ref.shape) - 2) + (pl.ds(start, n),)]


def _ffn_kernel(x_ref, mod_ref, ng_ref, wgu_ref, wdn_ref, fg_ref, o_ref, *, final_norm, per_row_mod):
    n_sub = SUB_TILES
    rows = x_ref.shape[-2] // n_sub
    d_ff = wdn_ref.shape[0]
    bounds = _ff_bounds(d_ff)

    def tile(i, carry):
        x = _rows(x_ref, i, rows)[...]
        mod = _rows(mod_ref, i, rows) if per_row_mod else mod_ref
        h = _rms_mod(x, ng_ref[...], mod[0:1], mod[1:2])
        hb = _flat(h).astype(BF16)
        acc = None
        for c0, c1 in zip(bounds[:-1], bounds[1:]):
            a = _dot(hb, wgu_ref[:, c0:c1])
            b = _dot(hb, wgu_ref[:, d_ff + c0:d_ff + c1])
            g = (a * jax.nn.sigmoid(a)) * b
            part = _dot(g.astype(BF16), wdn_ref[c0:c1, :])
            acc = part if acc is None else acc + part
        y = x + (0.5 * mod[2:3]) * acc.reshape(x.shape)
        if final_norm:
            ms = jnp.mean(y * y, axis=-1, keepdims=True)
            y = y * lax.rsqrt(ms + EPS) * fg_ref[...]
        _rows(o_ref, i, rows)[...] = y
        return carry

    lax.fori_loop(0, n_sub, tile, 0)


def _const_spec(shape, index_map):
    return pl.BlockSpec(shape, index_map, pipeline_mode=pl.Buffered(1))


def _ffn_call(x, mod, ng, wgu, wdn, fg, *, l, sub, sample, final_norm):
    d = x.shape[-1]
    two_ff = wgu.shape[-1]
    if sample:
        steps, batch, _ = x.shape
        bt = STEP_TOKENS // steps
        grid = (batch // bt,)
        x_spec = pl.BlockSpec((steps, bt, d), lambda i: (0, i, 0))
        mod_spec = pl.BlockSpec((None, N_SUB, bt, d), lambda i: (l, sub, i, 0))
        c2 = lambda i: (l, sub, 0, 0)
        c3 = lambda i: (l, 0, 0)
        c0 = lambda i: (0, 0)
    else:
        batch, seq, _ = x.shape
        grid = (batch, seq // STEP_TOKENS)
        x_spec = pl.BlockSpec((None, STEP_TOKENS, d), lambda b, t: (b, t, 0))
        mod_spec = pl.BlockSpec((None, None, None, N_SUB, d), lambda b, t: (l, b, sub, 0, 0))
        c2 = lambda b, t: (l, sub, 0, 0)
        c3 = lambda b, t: (l, 0, 0)
        c0 = lambda b, t: (0, 0)
    return pl.pallas_call(
        functools.partial(_ffn_kernel, final_norm=final_norm, per_row_mod=sample),
        grid=grid,
        in_specs=[
            x_spec,
            mod_spec,
            pl.BlockSpec((None, None, 1, d), c2),
            _const_spec((None, d, two_ff), c3),
            _const_spec((None, two_ff // 2, d), c3),
            pl.BlockSpec((1, d), c0),
        ],
        out_specs=x_spec,
        out_shape=jax.ShapeDtypeStruct(x.shape, F32),
        compiler_params=pltpu.CompilerParams(
            dimension_semantics=("arbitrary",) * len(grid), vmem_limit_bytes=VMEM_LIMIT),
        name=("ffn_sample" if sample else "ffn_prompt"),
    )(x, mod, ng, wgu, wdn, fg)


def _mixer_core(x, mod_ref, ng_ref, win_ref, woa_ref, lng_ref, wob_ref, poolw_ref, pscale_ref,
                wo_ref, conv_fn, gate_fn, pool_fn):
    d_conv = woa_ref.shape[0]
    d_gmlp = wob_ref.shape[0]
    d_pool = poolw_ref.shape[0] * poolw_ref.shape[1]
    d_model = wo_ref.shape[0]
    h = _rms_mod(x, ng_ref[...], mod_ref[0:1], mod_ref[1:2])
    hb = _flat(h).astype(BF16)

    win = lambda lo, n: win_ref[:, lo:lo + n]
    o_b = 3 * d_conv
    o_c = o_b + 2 * d_gmlp
    o_g = o_c + d_pool

    pa = _dot(hb, win(0, 3 * d_conv))
    xa = pa[:, :d_conv]
    bg = pa[:, d_conv:2 * d_conv]
    cg = pa[:, 2 * d_conv:]
    conv = conv_fn(cg * xa)
    a_in = (bg * conv).astype(BF16)

    puv = _dot(hb, win(o_b, 2 * d_gmlp))
    y_a = _dot(a_in, woa_ref[...])
    u = _gelu(puv[:, :d_gmlp])
    v = _gelu(puv[:, d_gmlp:])
    mu = jnp.mean(v, axis=-1, keepdims=True)
    vc = v - mu
    var = jnp.mean(vc * vc, axis=-1, keepdims=True)
    v = vc * lax.rsqrt(var + EPS) * lng_ref[...]

    p = _dot(hb, win(o_c, d_pool))
    merged = jax.nn.sigmoid(_dot(hb, win(o_g, d_model))) * y_a
    sg = gate_fn(v)
    gate_b = _dot(hb, win(o_g + d_model, d_model))

    diff = (pool_fn(p) - p).astype(BF16)
    gdim = poolw_ref.shape[1]
    y_b = _dot((u * sg).astype(BF16), wob_ref[...])
    y_c = jnp.concatenate(
        [_dot(diff[:, g * gdim:(g + 1) * gdim], poolw_ref[g]) for g in range(poolw_ref.shape[0])],
        axis=1) * pscale_ref[...]
    gate_c = _dot(hb, win(o_g + 2 * d_model, d_model))
    merged = merged + jax.nn.sigmoid(gate_b) * y_b + jax.nn.sigmoid(gate_c) * y_c
    m = _dot(merged.astype(BF16), wo_ref[...])
    return x + mod_ref[2:3] * m.reshape(x.shape)


def _mixer_prompt_kernel(x_ref, mod_ref, ng_ref, win_ref, cw_ref, woa_ref, lng_ref, wpair_ref,
                         sgb_ref, wob_ref, poolw_ref, pscale_ref, wo_ref,
                         o_ref, ctail_ref, ptail_ref, zc_ref, pc_ref, pd_ref):
    tm = x_ref.shape[0] // SUB_TILES
    zh = zc_ref.shape[1] - tm
    ph = pc_ref.shape[1] - tm
    lax.fori_loop(0, SUB_TILES, functools.partial(
        _mixer_prompt_tile, x_ref, mod_ref, ng_ref, win_ref, cw_ref, woa_ref, lng_ref, wpair_ref,
        sgb_ref, wob_ref, poolw_ref, pscale_ref, wo_ref, o_ref, ctail_ref, ptail_ref, zc_ref,
        pc_ref, pd_ref, tm, zh, ph), 0)


def _mixer_prompt_tile(x_ref, mod_ref, ng_ref, win_ref, cw_ref, woa_ref, lng_ref, wpair_ref,
                       sgb_ref, wob_ref, poolw_ref, pscale_ref, wo_ref, o_ref, ctail_ref,
                       ptail_ref, zc_ref, pc_ref, pd_ref, tm, zh, ph, i, carry):
    t = pl.program_id(1) * SUB_TILES + i

    @pl.when(t == 0)
    def _():
        zc_ref[:, 0:zh, :] = jnp.zeros((zc_ref.shape[0], zh, LANES), F32)
        pc_ref[:, 0:ph, :] = jnp.zeros((pc_ref.shape[0], ph, LANES), F32)

    def conv_fn(z):
        outs = []
        for g in range(zc_ref.shape[0]):
            ln = slice(g * LANES, (g + 1) * LANES)
            zg = z[:, ln]
            zc_ref[g, zh:, :] = zg
            acc = zg * cw_ref[CONV_W - 1:CONV_W, ln]
            for k in range(1, CONV_W):
                acc = acc + zc_ref[g, zh - k:zh - k + tm, :] * cw_ref[CONV_W - 1 - k:CONV_W - k, ln]
            tail = zg[tm - SUBLANES:]
            zc_ref[g, zh - SUBLANES:zh, :] = tail
            ctail_ref[:, ln] = tail
            outs.append(acc)
        return jnp.concatenate(outs, axis=1)

    def gate_fn(v):
        lane = lax.broadcasted_iota(jnp.int32, (CHUNK, LANES), 1)
        low = lane < (LANES // 2)
        n_blk = v.shape[1] // LANES
        rows = []
        for c in range(tm // CHUNK):
            vc = v[c * CHUNK:(c + 1) * CHUNK]
            outs = []
            for j in range(n_blk):
                blk = vc[:, j * LANES:(j + 1) * LANES]
                rhs = jnp.concatenate(
                    [jnp.where(low, blk, 0.0), jnp.where(low, 0.0, blk)], axis=0).astype(BF16)
                outs.append(_dot(wpair_ref[j], rhs))
            rows.append(jnp.concatenate(outs, axis=1) + sgb_ref[...])
        return jnp.concatenate(rows, axis=0)

    def pool_fn(p):
        pos = t * tm + lax.broadcasted_iota(jnp.int32, (MAX_WIN, 1), 0)
        outs = []
        buf = 0
        for gi, w in enumerate(POOL_WINDOWS):
            ln = slice(gi * LANES, (gi + 1) * LANES)
            pg = p[:, ln]
            pc_ref[gi, ph:, :] = pg
            src = pc_ref.at[gi]
            k = 1
            lo = 0
            while 2 * k < w:
                lo += SUBLANES
                dst = pd_ref.at[buf]
                buf += 1
                dst[lo:, :] = src[lo:, :] + src[lo - k:ph + tm - k, :]
                src = dst
                k *= 2
            s = src[ph:, :] + src[ph - k:ph + tm - k, :]
            head = s[:MAX_WIN] * (1.0 / jnp.minimum(pos + 1, w).astype(F32))
            outs.append(jnp.concatenate([head, s[MAX_WIN:] * (1.0 / w)], axis=0))
            tail = pg[tm - MAX_WIN:]
            pc_ref[gi, ph - MAX_WIN:ph, :] = tail
            ptail_ref[:, ln] = tail
        return jnp.concatenate(outs, axis=1)

    _rows(o_ref, i, tm)[...] = _mixer_core(
        _rows(x_ref, i, tm)[...], mod_ref, ng_ref, win_ref, woa_ref, lng_ref, wob_ref,
        poolw_ref, pscale_ref, wo_ref, conv_fn, gate_fn, pool_fn)
    return carry


def _mixer_sample_kernel(x_ref, mod_ref, ng_ref, win_ref, cw_ref, woa_ref, lng_ref, sgc_ref,
                         sgb_ref, wob_ref, poolw_ref, pscale_ref, wo_ref, hc_ref, hp_ref,
                         o_ref, nconv_ref, npool_ref, vopen_ref):
    bt = x_ref.shape[1] // SUB_TILES
    lax.fori_loop(0, SUB_TILES, functools.partial(
        _mixer_sample_tile, x_ref, mod_ref, ng_ref, win_ref, cw_ref, woa_ref, lng_ref, sgc_ref,
        sgb_ref, wob_ref, poolw_ref, pscale_ref, wo_ref, hc_ref, hp_ref, o_ref, nconv_ref,
        npool_ref, vopen_ref, bt), 0)


def _mixer_sample_tile(x_ref, mod_ref, ng_ref, win_ref, cw_ref, woa_ref, lng_ref, sgc_ref,
                       sgb_ref, wob_ref, poolw_ref, pscale_ref, wo_ref, hc_ref, hp_ref, o_ref,
                       nconv_ref, npool_ref, vopen_ref, bt, i, carry):
    x_ref, mod_ref, hc_ref, hp_ref, o_ref, nconv_ref, npool_ref, vopen_ref = (
        _rows(r, i, bt) for r in (x_ref, mod_ref, hc_ref, hp_ref, o_ref, nconv_ref, npool_ref,
                                  vopen_ref))
    steps = x_ref.shape[0]

    def conv_fn(z):
        z3 = z.reshape(steps, bt, z.shape[-1])
        ext = jnp.concatenate([hc_ref[...], z3], axis=0)
        nconv_ref[...] = ext[steps:]
        conv = sum(ext[k:k + steps] * cw_ref[k:k + 1] for k in range(CONV_W))
        return _flat(conv)

    def gate_fn(v):
        v3 = v.reshape(steps, bt, v.shape[-1])
        vopen_ref[...] = v3
        outs = []
        for tt in range(steps):
            acc = sgb_ref[tt:tt + 1] + sgc_ref[tt * steps:tt * steps + 1] * v3[0]
            for s in range(1, tt + 1):
                acc = acc + sgc_ref[tt * steps + s:tt * steps + s + 1] * v3[s]
            outs.append(acc)
        return _flat(jnp.stack(outs, axis=0))

    def pool_fn(p):
        p3 = p.reshape(steps, bt, p.shape[-1])
        ext = jnp.concatenate([hp_ref[...], p3], axis=0)
        hist = hp_ref.shape[0]
        npool_ref[...] = ext[steps:]
        outs = []
        for gi, w in enumerate(POOL_WINDOWS):
            s = ext[:, :, gi * LANES:(gi + 1) * LANES]
            k = 1
            while k < w:
                s = s[k:] + s[:-k]
                k *= 2
            first = hist - (w - 1)
            outs.append(jnp.stack(
                [s[first + tt] * (1.0 / min(PAST_LEN + tt + 1, w)) for tt in range(steps)], axis=0))
        return _flat(jnp.concatenate(outs, axis=-1))

    o_ref[...] = _mixer_core(x_ref[...], mod_ref, ng_ref, win_ref, woa_ref, lng_ref, wob_ref,
                             poolw_ref, pscale_ref, wo_ref, conv_fn, gate_fn, pool_fn)
    return carry


def _mixer_prompt_call(x, mod, ng, win, cw, woa, lng, wpair, sgb, wob, poolw, pscale, wo, *, l):
    batch, seq, d = x.shape
    d_conv = cw.shape[-1]
    d_pool = poolw.shape[2] * poolw.shape[1]
    sub = 1
    tm = TOKEN_TILE
    n_pool_tmp = sum(max(w.bit_length() - 2, 0) for w in POOL_WINDOWS)
    cl = lambda *tail: (lambda b, t: (l,) + tail)
    full = lambda a: _const_spec((None,) + a.shape[1:], cl(*([0] * (a.ndim - 1))))
    return pl.pallas_call(
        _mixer_prompt_kernel,
        grid=(batch, seq // STEP_TOKENS),
        in_specs=[
            pl.BlockSpec((None, STEP_TOKENS, d), lambda b, t: (b, t, 0)),
            pl.BlockSpec((None, None, None, N_SUB, d), lambda b, t: (l, b, sub, 0, 0)),
            pl.BlockSpec((None, None, 1, d), lambda b, t: (l, sub, 0, 0)),
            full(win), full(cw), full(woa), full(lng), full(wpair), full(sgb), full(wob),
            full(poolw), full(pscale), full(wo),
        ],
        out_specs=[
            pl.BlockSpec((None, STEP_TOKENS, d), lambda b, t: (b, t, 0)),
            pl.BlockSpec((None, SUBLANES, d_conv), lambda b, t: (b, 0, 0)),
            pl.BlockSpec((None, MAX_WIN, d_pool), lambda b, t: (b, 0, 0)),
        ],
        out_shape=[
            jax.ShapeDtypeStruct(x.shape, F32),
            jax.ShapeDtypeStruct((batch, SUBLANES, d_conv), F32),
            jax.ShapeDtypeStruct((batch, MAX_WIN, d_pool), F32),
        ],
        scratch_shapes=[
            pltpu.VMEM((d_conv // LANES, SUBLANES + tm, LANES), F32),
            pltpu.VMEM((d_pool // LANES, 2 * MAX_WIN + tm, LANES), F32),
            pltpu.VMEM((n_pool_tmp, 2 * MAX_WIN + tm, LANES), F32),
        ],
        compiler_params=pltpu.CompilerParams(
            dimension_semantics=("arbitrary", "arbitrary"), vmem_limit_bytes=VMEM_LIMIT),
        name="mixer_prompt",
    )(x, mod, ng, win, cw, woa, lng, wpair, sgb, wob, poolw, pscale, wo)


def _mixer_sample_call(x, mod, ng, win, cw, woa, lng, sgc, sgb, wob, poolw, pscale, wo,
                       hconv, hpool, *, l):
    steps, batch, d = x.shape
    d_conv = cw.shape[-1]
    d_gmlp = wob.shape[1]
    d_pool = poolw.shape[2] * poolw.shape[1]
    sub = 1
    bt = STEP_TOKENS // steps
    cl = lambda *tail: (lambda i: (l,) + tail)
    full = lambda a: _const_spec((None,) + a.shape[1:], cl(*([0] * (a.ndim - 1))))
    n_hc, n_hp = hconv.shape[1], hpool.shape[1]
    return pl.pallas_call(
        _mixer_sample_kernel,
        grid=(batch // bt,),
        in_specs=[
            pl.BlockSpec((steps, bt, d), lambda i: (0, i, 0)),
            pl.BlockSpec((None, N_SUB, bt, d), lambda i: (l, sub, i, 0)),
            pl.BlockSpec((None, None, 1, d), lambda i: (l, sub, 0, 0)),
            full(win), full(cw), full(woa), full(lng), full(sgc), full(sgb), full(wob),
            full(poolw), full(pscale), full(wo),
            pl.BlockSpec((None, n_hc, bt, d_conv), lambda i: (l, 0, i, 0)),
            pl.BlockSpec((None, n_hp, bt, d_pool), lambda i: (l, 0, i, 0)),
        ],
        out_specs=[
            pl.BlockSpec((steps, bt, d), lambda i: (0, i, 0)),
            pl.BlockSpec((n_hc, bt, d_conv), lambda i: (0, i, 0)),
            pl.BlockSpec((n_hp, bt, d_pool), lambda i: (0, i, 0)),
            pl.BlockSpec((steps, bt, d_gmlp), lambda i: (0, i, 0)),
        ],
        out_shape=[
            jax.ShapeDtypeStruct(x.shape, F32),
            jax.ShapeDtypeStruct((n_hc, batch, d_conv), F32),
            jax.ShapeDtypeStruct((n_hp, batch, d_pool), F32),
            jax.ShapeDtypeStruct((steps, batch, d_gmlp), F32),
        ],
        compiler_params=pltpu.CompilerParams(
            dimension_semantics=("arbitrary",), vmem_limit_bytes=VMEM_LIMIT),
        name="mixer_sample",
    )(x, mod, ng, win, cw, woa, lng, sgc, sgb, wob, poolw, pscale, wo, hconv, hpool)


def kernel(x_prompt, x_sample, state_conv, state_pool, c_prompt, c_sample, norm_g, w_ada, b_ada,
           w1_gu, w1_dn, w2_gu, w2_dn, w_in, conv_w, w_out_a, ln_g, w_s, b_s, w_out_b,
           pool_w, pool_scale, w_o, final_norm_g):
    depth = w_in.shape[0]
    batch, seq, d = x_prompt.shape
    dec_batch, steps, _ = x_sample.shape
    d_gmlp = ln_g.shape[-1]
    head_dim = d_gmlp // G_HEADS
    assert seq % STEP_TOKENS == 0 and TOKEN_TILE % CHUNK == 0 and TOKEN_TILE % (steps * SUBLANES) == 0
    assert dec_batch % (STEP_TOKENS // steps) == 0 and steps <= CHUNK and 2 * head_dim == LANES

    c_all = jnp.concatenate([c_sample, c_prompt], axis=0)
    mod_all = _ada_call(c_all, w_ada, b_ada)
    mod_p = mod_all[:, :, dec_batch:, :].transpose(0, 2, 1, 3).reshape(depth, batch, N_SUB, 3, d)

    w1gu, w1dn = w1_gu.astype(BF16), w1_dn.astype(BF16)
    w2gu, w2dn = w2_gu.astype(BF16), w2_dn.astype(BF16)
    win = w_in.astype(BF16)
    woa = w_out_a.astype(BF16)
    wob = w_out_b.astype(BF16)
    wo = w_o.astype(BF16)
    poolw = pool_w.astype(BF16)
    ng = norm_g.reshape(depth, N_SUB, 1, d)
    lng = ln_g.reshape(depth, 1, d_gmlp)
    pscale = pool_scale.reshape(depth, 1, d)
    fg = final_norm_g.reshape(1, d)
    w_tril = w_s * jnp.tril(jnp.ones((CHUNK, CHUNK), w_s.dtype))
    wpair = w_tril.reshape(depth, G_HEADS // 2, 2, CHUNK, CHUNK).transpose(0, 1, 3, 2, 4)
    wpair = wpair.reshape(depth, G_HEADS // 2, CHUNK, 2 * CHUNK).astype(BF16)
    sgb = jnp.repeat(b_s.transpose(0, 2, 1), head_dim, axis=-1)
    sgc = jnp.repeat(w_tril[:, :, :steps, :steps].transpose(0, 2, 3, 1), head_dim, axis=-1)
    sgc = sgc.reshape(depth, steps * steps, d_gmlp)

    xp = x_prompt
    xs = x_sample.transpose(1, 0, 2)
    hconv = state_conv.transpose(0, 2, 1, 3)
    hpool = state_pool.transpose(0, 2, 1, 3)

    conv_p, conv_s, pool_p, pool_s, v_s = [], [], [], [], []
    for l in range(depth):
        last = l == depth - 1
        xp = _ffn_call(xp, mod_p, ng, w1gu, w1dn, fg, l=l, sub=0, sample=False, final_norm=False)
        xs = _ffn_call(xs, mod_all, ng, w1gu, w1dn, fg, l=l, sub=0, sample=True, final_norm=False)
        xp, ctail, ptail = _mixer_prompt_call(xp, mod_p, ng, win, conv_w, woa, lng, wpair, sgb,
                                              wob, poolw, pscale, wo, l=l)
        xs, ncs, nps, vs = _mixer_sample_call(xs, mod_all, ng, win, conv_w, woa, lng, sgc, sgb,
                                              wob, poolw, pscale, wo, hconv, hpool, l=l)
        xp = _ffn_call(xp, mod_p, ng, w2gu, w2dn, fg, l=l, sub=2, sample=False, final_norm=last)
        xs = _ffn_call(xs, mod_all, ng, w2gu, w2dn, fg, l=l, sub=2, sample=True, final_norm=last)
        conv_p.append(ctail[:, SUBLANES - (CONV_W - 1):])
        pool_p.append(ptail[:, 1:])
        conv_s.append(ncs.transpose(1, 0, 2))
        pool_s.append(nps.transpose(1, 0, 2))
        v_s.append(vs.transpose(1, 0, 2))

    return (xp, xs.transpose(1, 0, 2), jnp.stack(conv_p), jnp.stack(conv_s), jnp.stack(pool_p),
            jnp.stack(pool_s), jnp.stack(v_s))
```

```python
import functools
import math

import jax
import jax.numpy as jnp
from jax import lax
from jax.experimental import pallas as pl
from jax.experimental.pallas import tpu as pltpu

F32 = jnp.float32
BF16 = jnp.bfloat16

EPS = 1e-6
N_SUB = 3
CONV_W = 3
G_HEADS = 8
CHUNK = 128
POOL_WINDOWS = (2, 4, 8, 16)
MAX_WIN = max(POOL_WINDOWS)
PAST_LEN = 16384

LANES = 128
SUBLANES = 8
MXU_DIM = 256
TOKEN_TILE = 512
SUB_TILES = 1
STEP_TOKENS = TOKEN_TILE * SUB_TILES
FF_CHUNKS = 2
VMEM_LIMIT = 56 * 1024 * 1024


def _flat(a):
    return a.reshape(-1, a.shape[-1])


def _rms_mod(x, g, shift, scale):
    ms = jnp.mean(x * x, axis=-1, keepdims=True)
    y = x * lax.rsqrt(ms + EPS) * g
    return y * (1.0 + scale) + shift


_dot = functools.partial(jnp.dot, preferred_element_type=F32)


def _gelu(x):
    return 0.5 * x * (1.0 + lax.erf(x * (1.0 / math.sqrt(2.0))))


def _ada_kernel(c_ref, w_ref, b_ref, o_ref):
    c = c_ref[...]
    sc = (c * jax.nn.sigmoid(c)).astype(BF16)
    o_ref[...] = _dot(sc, w_ref[...].astype(BF16)) + b_ref[...]


def _ada_call(c_all, w_ada, b_ada):
    depth, d, cols = w_ada.shape
    n_mod = cols // d
    rows = c_all.shape[0]
    return pl.pallas_call(
        _ada_kernel,
        grid=(depth, n_mod),
        in_specs=[
            pl.BlockSpec((rows, d), lambda l, j: (0, 0)),
            pl.BlockSpec((None, d, d), lambda l, j: (l, 0, j)),
            pl.BlockSpec((None, None, 1, d), lambda l, j: (l, j, 0, 0)),
        ],
        out_specs=pl.BlockSpec((None, None, rows, d), lambda l, j: (l, j, 0, 0)),
        out_shape=jax.ShapeDtypeStruct((depth, n_mod, rows, d), F32),
        compiler_params=pltpu.CompilerParams(
            dimension_semantics=("arbitrary", "arbitrary"), vmem_limit_bytes=VMEM_LIMIT),
        name="ada_mod",
    )(c_all, w_ada, b_ada.reshape(depth, n_mod, 1, d))


def _ff_bounds(d_ff):
    tiles = d_ff // MXU_DIM
    per = -(-tiles // FF_CHUNKS)
    return [min(i * per, tiles) * MXU_DIM for i in range(FF_CHUNKS + 1)]


def _rows(ref, i, n):
    start = pl.multiple_of(i * n, n)
    return ref.at[(slice(None),) * (len(ref.shape) - 2) + (pl.ds(start, n),)]


def _ffn_kernel(x_ref, mod_ref, ng_ref, wgu_ref, wdn_ref, fg_ref, o_ref, *, final_norm, per_row_mod):
    n_sub = SUB_TILES
    rows = x_ref.shape[-2] // n_sub
    d_ff = wdn_ref.shape[0]
    bounds = _ff_bounds(d_ff)

    def tile(i, carry):
        x = _rows(x_ref, i, rows)[...]
        mod = _rows(mod_ref, i, rows) if per_row_mod else mod_ref
        h = _rms_mod(x, ng_ref[...], mod[0:1], mod[1:2])
        hb = _flat(h).astype(BF16)
        acc = None
        for c0, c1 in zip(bounds[:-1], bounds[1:]):
            a = _dot(hb, wgu_ref[:, c0:c1])
            b = _dot(hb, wgu_ref[:, d_ff + c0:d_ff + c1])
            g = (a * jax.nn.sigmoid(a)) * b
            part = _dot(g.astype(BF16), wdn_ref[c0:c1, :])
            acc = part if acc is None else acc + part
        y = x + (0.5 * mod[2:3]) * acc.reshape(x.shape)
        if final_norm:
            ms = jnp.mean(y * y, axis=-1, keepdims=True)
            y = y * lax.rsqrt(ms + EPS) * fg_ref[...]
        _rows(o_ref, i, rows)[...] = y
        return carry

    lax.fori_loop(0, n_sub, tile, 0)


def _const_spec(shape, index_map):
    return pl.BlockSpec(shape, index_map, pipeline_mode=pl.Buffered(1))


def _ffn_pair_kernel(xp_ref, xs_ref, modp_ref, mods_ref, ng_ref, wgu_ref, wdn_ref, fg_ref,
                     op_ref, os_ref, *, final_norm, prompt_steps):
    i = pl.program_id(0)

    @pl.when(i < prompt_steps)
    def _():
        _ffn_kernel(xp_ref, modp_ref, ng_ref, wgu_ref, wdn_ref, fg_ref, op_ref,
                    final_norm=final_norm, per_row_mod=False)

    @pl.when(i >= prompt_steps)
    def _():
        _ffn_kernel(xs_ref, mods_ref, ng_ref, wgu_ref, wdn_ref, fg_ref, os_ref,
                    final_norm=final_norm, per_row_mod=True)


def _ffn_call(xp, xs, mod_p, mod_all, ng, wgu, wdn, fg, *, l, sub, final_norm):
    batch, seq, d = xp.shape
    steps, dec_batch, _ = xs.shape
    two_ff = wgu.shape[-1]
    per_seq = seq // STEP_TOKENS
    prompt_steps = batch * per_seq
    bt = STEP_TOKENS // steps
    pi = lambda i: jnp.minimum(i, prompt_steps - 1)
    si = lambda i: jnp.maximum(i - prompt_steps, 0)
    xp_spec = pl.BlockSpec((None, STEP_TOKENS, d), lambda i: (pi(i) // per_seq, pi(i) % per_seq, 0))
    xs_spec = pl.BlockSpec((steps, bt, d), lambda i: (0, si(i), 0))
    return pl.pallas_call(
        functools.partial(_ffn_pair_kernel, final_norm=final_norm, prompt_steps=prompt_steps),
        grid=(prompt_steps + dec_batch // bt,),
        in_specs=[
            xp_spec,
            xs_spec,
            pl.BlockSpec((None, None, None, N_SUB, d), lambda i: (l, pi(i) // per_seq, sub, 0, 0)),
            pl.BlockSpec((None, N_SUB, bt, d), lambda i: (l, sub, si(i), 0)),
            pl.BlockSpec((None, None, 1, d), lambda i: (l, sub, 0, 0)),
            _const_spec((None, d, two_ff), lambda i: (l, 0, 0)),
            _const_spec((None, two_ff // 2, d), lambda i: (l, 0, 0)),
            pl.BlockSpec((1, d), lambda i: (0, 0)),
        ],
        out_specs=[xp_spec, xs_spec],
        out_shape=[jax.ShapeDtypeStruct(xp.shape, F32), jax.ShapeDtypeStruct(xs.shape, F32)],
        compiler_params=pltpu.CompilerParams(
            dimension_semantics=("arbitrary",), vmem_limit_bytes=VMEM_LIMIT),
        name="ffn",
    )(xp, xs, mod_p, mod_all, ng, wgu, wdn, fg)


def _mixer_core(x, mod_ref, ng_ref, win_ref, woa_ref, lng_ref, wob_ref, poolw_ref, pscale_ref,
                wo_ref, conv_fn, gate_fn, pool_fn):
    d_conv = woa_ref.shape[0]
    d_gmlp = wob_ref.shape[0]
    d_pool = poolw_ref.shape[0] * poolw_ref.shape[1]
    d_model = wo_ref.shape[0]
    h = _rms_mod(x, ng_ref[...], mod_ref[0:1], mod_ref[1:2])
    hb = _flat(h).astype(BF16)

    win = lambda lo, n: win_ref[:, lo:lo + n]
    o_b = 3 * d_conv
    o_c = o_b + 2 * d_gmlp
    o_g = o_c + d_pool

    pa = _dot(hb, win(0, 3 * d_conv))
    xa = pa[:, :d_conv]
    bg = pa[:, d_conv:2 * d_conv]
    cg = pa[:, 2 * d_conv:]
    conv = conv_fn(cg * xa)
    a_in = (bg * conv).astype(BF16)

    puv = _dot(hb, win(o_b, 2 * d_gmlp))
    y_a = _dot(a_in, woa_ref[...])
    u = _gelu(puv[:, :d_gmlp])
    v = _gelu(puv[:, d_gmlp:])
    mu = jnp.mean(v, axis=-1, keepdims=True)
    vc = v - mu
    var = jnp.mean(vc * vc, axis=-1, keepdims=True)
    v = vc * lax.rsqrt(var + EPS) * lng_ref[...]

    p = _dot(hb, win(o_c, d_pool))
    merged = jax.nn.sigmoid(_dot(hb, win(o_g, d_model))) * y_a
    sg = gate_fn(v)
    gate_b = _dot(hb, win(o_g + d_model, d_model))

    diff = (pool_fn(p) - p).astype(BF16)
    gdim = poolw_ref.shape[1]
    y_b = _dot((u * sg).astype(BF16), wob_ref[...])
    y_c = jnp.concatenate(
        [_dot(diff[:, g * gdim:(g + 1) * gdim], poolw_ref[g]) for g in range(poolw_ref.shape[0])],
        axis=1) * pscale_ref[...]
    gate_c = _dot(hb, win(o_g + 2 * d_model, d_model))
    merged = merged + jax.nn.sigmoid(gate_b) * y_b + jax.nn.sigmoid(gate_c) * y_c
    m = _dot(merged.astype(BF16), wo_ref[...])
    return x + mod_ref[2:3] * m.reshape(x.shape)


def _mixer_prompt_kernel(x_ref, mod_ref, ng_ref, win_ref, cw_ref, woa_ref, lng_ref, wpair_ref,
                         sgb_ref, wob_ref, poolw_ref, pscale_ref, wo_ref,
                         o_ref, ctail_ref, ptail_ref, zc_ref, pc_ref, pd_ref, *, seq_step):
    tm = x_ref.shape[0] // SUB_TILES
    zh = zc_ref.shape[1] - tm
    ph = pc_ref.shape[1] - tm
    lax.fori_loop(0, SUB_TILES, functools.partial(
        _mixer_prompt_tile, x_ref, mod_ref, ng_ref, win_ref, cw_ref, woa_ref, lng_ref, wpair_ref,
        sgb_ref, wob_ref, poolw_ref, pscale_ref, wo_ref, o_ref, ctail_ref, ptail_ref, zc_ref,
        pc_ref, pd_ref, tm, zh, ph, seq_step), 0)


def _mixer_prompt_tile(x_ref, mod_ref, ng_ref, win_ref, cw_ref, woa_ref, lng_ref, wpair_ref,
                       sgb_ref, wob_ref, poolw_ref, pscale_ref, wo_ref, o_ref, ctail_ref,
                       ptail_ref, zc_ref, pc_ref, pd_ref, tm, zh, ph, seq_step, i, carry):
    t = seq_step * SUB_TILES + i

    @pl.when(t == 0)
    def _():
        zc_ref[:, 0:zh, :] = jnp.zeros((zc_ref.shape[0], zh, LANES), F32)
        pc_ref[:, 0:ph, :] = jnp.zeros((pc_ref.shape[0], ph, LANES), F32)

    def conv_fn(z):
        outs = []
        for g in range(zc_ref.shape[0]):
            ln = slice(g * LANES, (g + 1) * LANES)
            zg = z[:, ln]
            zc_ref[g, zh:, :] = zg
            acc = zg * cw_ref[CONV_W - 1:CONV_W, ln]
            for k in range(1, CONV_W):
                acc = acc + zc_ref[g, zh - k:zh - k + tm, :] * cw_ref[CONV_W - 1 - k:CONV_W - k, ln]
            tail = zg[tm - SUBLANES:]
            zc_ref[g, zh - SUBLANES:zh, :] = tail
            ctail_ref[:, ln] = tail
            outs.append(acc)
        return jnp.concatenate(outs, axis=1)

    def gate_fn(v):
        lane = lax.broadcasted_iota(jnp.int32, (CHUNK, LANES), 1)
        low = lane < (LANES // 2)
        n_blk = v.shape[1] // LANES
        rows = []
        for c in range(tm // CHUNK):
            vc = v[c * CHUNK:(c + 1) * CHUNK]
            outs = []
            for j in range(n_blk):
                blk = vc[:, j * LANES:(j + 1) * LANES]
                rhs = jnp.concatenate(
                    [jnp.where(low, blk, 0.0), jnp.where(low, 0.0, blk)], axis=0).astype(BF16)
                outs.append(_dot(wpair_ref[j], rhs))
            rows.append(jnp.concatenate(outs, axis=1) + sgb_ref[...])
        return jnp.concatenate(rows, axis=0)

    def pool_fn(p):
        pos = t * tm + lax.broadcasted_iota(jnp.int32, (MAX_WIN, 1), 0)
        outs = []
        buf = 0
        for gi, w in enumerate(POOL_WINDOWS):
            ln = slice(gi * LANES, (gi + 1) * LANES)
            pg = p[:, ln]
            pc_ref[gi, ph:, :] = pg
            src = pc_ref.at[gi]
            k = 1
            lo = 0
            while 2 * k < w:
                lo += SUBLANES
                dst = pd_ref.at[buf]
                buf += 1
                dst[lo:, :] = src[lo:, :] + src[lo - k:ph + tm - k, :]
                src = dst
                k *= 2
            s = src[ph:, :] + src[ph - k:ph + tm - k, :]
            head = s[:MAX_WIN] * (1.0 / jnp.minimum(pos + 1, w).astype(F32))
            outs.append(jnp.concatenate([head, s[MAX_WIN:] * (1.0 / w)], axis=0))
            tail = pg[tm - MAX_WIN:]
            pc_ref[gi, ph - MAX_WIN:ph, :] = tail
            ptail_ref[:, ln] = tail
        return jnp.concatenate(outs, axis=1)

    _rows(o_ref, i, tm)[...] = _mixer_core(
        _rows(x_ref, i, tm)[...], mod_ref, ng_ref, win_ref, woa_ref, lng_ref, wob_ref,
        poolw_ref, pscale_ref, wo_ref, conv_fn, gate_fn, pool_fn)
    return carry


def _mixer_sample_kernel(x_ref, mod_ref, ng_ref, win_ref, cw_ref, woa_ref, lng_ref, sgc_ref,
                         sgb_ref, wob_ref, poolw_ref, pscale_ref, wo_ref, hc_ref, hp_ref,
                         o_ref, nconv_ref, npool_ref, vopen_ref):
    bt = x_ref.shape[1] // SUB_TILES
    lax.fori_loop(0, SUB_TILES, functools.partial(
        _mixer_sample_tile, x_ref, mod_ref, ng_ref, win_ref, cw_ref, woa_ref, lng_ref, sgc_ref,
        sgb_ref, wob_ref, poolw_ref, pscale_ref, wo_ref, hc_ref, hp_ref, o_ref, nconv_ref,
        npool_ref, vopen_ref, bt), 0)


def _mixer_sample_tile(x_ref, mod_ref, ng_ref, win_ref, cw_ref, woa_ref, lng_ref, sgc_ref,
                       sgb_ref, wob_ref, poolw_ref, pscale_ref, wo_ref, hc_ref, hp_ref, o_ref,
                       nconv_ref, npool_ref, vopen_ref, bt, i, carry):
    x_ref, mod_ref, hc_ref, hp_ref, o_ref, nconv_ref, npool_ref, vopen_ref = (
        _rows(r, i, bt) for r in (x_ref, mod_ref, hc_ref, hp_ref, o_ref, nconv_ref, npool_ref,
                                  vopen_ref))
    steps = x_ref.shape[0]

    def conv_fn(z):
        z3 = z.reshape(steps, bt, z.shape[-1])
        ext = jnp.concatenate([hc_ref[...], z3], axis=0)
        nconv_ref[...] = ext[steps:]
        conv = sum(ext[k:k + steps] * cw_ref[k:k + 1] for k in range(CONV_W))
        return _flat(conv)

    def gate_fn(v):
        v3 = v.reshape(steps, bt, v.shape[-1])
        vopen_ref[...] = v3
        outs = []
        for tt in range(steps):
            acc = sgb_ref[tt:tt + 1] + sgc_ref[tt * steps:tt * steps + 1] * v3[0]
            for s in range(1, tt + 1):
                acc = acc + sgc_ref[tt * steps + s:tt * steps + s + 1] * v3[s]
            outs.append(acc)
        return _flat(jnp.stack(outs, axis=0))

    def pool_fn(p):
        p3 = p.reshape(steps, bt, p.shape[-1])
        ext = jnp.concatenate([hp_ref[...], p3], axis=0)
        hist = hp_ref.shape[0]
        npool_ref[...] = ext[steps:]
        outs = []
        for gi, w in enumerate(POOL_WINDOWS):
            s = ext[:, :, gi * LANES:(gi + 1) * LANES]
            k = 1
            while k < w:
                s = s[k:] + s[:-k]
                k *= 2
            first = hist - (w - 1)
            outs.append(jnp.stack(
                [s[first + tt] * (1.0 / min(PAST_LEN + tt + 1, w)) for tt in range(steps)], axis=0))
        return _flat(jnp.concatenate(outs, axis=-1))

    o_ref[...] = _mixer_core(x_ref[...], mod_ref, ng_ref, win_ref, woa_ref, lng_ref, wob_ref,
                             poolw_ref, pscale_ref, wo_ref, conv_fn, gate_fn, pool_fn)
    return carry


def _mixer_pair_kernel(xp_ref, xs_ref, modp_ref, mods_ref, ng_ref, win_ref, cw_ref, woa_ref, lng_ref,
                       wpair_ref, sgc_ref, sgb_ref, wob_ref, poolw_ref, pscale_ref, wo_ref,
                       hc_ref, hp_ref, op_ref, ctail_ref, ptail_ref, os_ref, nconv_ref, npool_ref,
                       vopen_ref, zc_ref, pc_ref, pd_ref, *, prompt_steps, per_seq):
    i = pl.program_id(0)

    @pl.when(i < prompt_steps)
    def _():
        _mixer_prompt_kernel(xp_ref, modp_ref, ng_ref, win_ref, cw_ref, woa_ref, lng_ref, wpair_ref,
                             sgb_ref, wob_ref, poolw_ref, pscale_ref, wo_ref, op_ref, ctail_ref,
                             ptail_ref, zc_ref, pc_ref, pd_ref, seq_step=i % per_seq)

    @pl.when(i >= prompt_steps)
    def _():
        _mixer_sample_kernel(xs_ref, mods_ref, ng_ref, win_ref, cw_ref, woa_ref, lng_ref, sgc_ref,
                             sgb_ref, wob_ref, poolw_ref, pscale_ref, wo_ref, hc_ref, hp_ref,
                             os_ref, nconv_ref, npool_ref, vopen_ref)


def _mixer_call(xp, xs, mod_p, mod_all, ng, win, cw, woa, lng, wpair, sgc, sgb, wob, poolw, pscale,
                wo, hconv, hpool, *, l):
    batch, seq, d = xp.shape
    steps, dec_batch, _ = xs.shape
    d_conv = cw.shape[-1]
    d_gmlp = wob.shape[1]
    d_pool = poolw.shape[2] * poolw.shape[1]
    sub = 1
    tm = TOKEN_TILE
    per_seq = seq // STEP_TOKENS
    prompt_steps = batch * per_seq
    bt = STEP_TOKENS // steps
    n_hc, n_hp = hconv.shape[1], hpool.shape[1]
    n_pool_tmp = sum(max(w.bit_length() - 2, 0) for w in POOL_WINDOWS)
    pi = lambda i: jnp.minimum(i, prompt_steps - 1)
    si = lambda i: jnp.maximum(i - prompt_steps, 0)
    full = lambda a: _const_spec((None,) + a.shape[1:], lambda i: (l,) + (0,) * (a.ndim - 1))
    xp_spec = pl.BlockSpec((None, STEP_TOKENS, d), lambda i: (pi(i) // per_seq, pi(i) % per_seq, 0))
    xs_spec = pl.BlockSpec((steps, bt, d), lambda i: (0, si(i), 0))
    return pl.pallas_call(
        functools.partial(_mixer_pair_kernel, prompt_steps=prompt_steps, per_seq=per_seq),
        grid=(prompt_steps + dec_batch // bt,),
        in_specs=[
            xp_spec,
            xs_spec,
            pl.BlockSpec((None, None, None, N_SUB, d), lambda i: (l, pi(i) // per_seq, sub, 0, 0)),
            pl.BlockSpec((None, N_SUB, bt, d), lambda i: (l, sub, si(i), 0)),
            pl.BlockSpec((None, None, 1, d), lambda i: (l, sub, 0, 0)),
            full(win), full(cw), full(woa), full(lng), full(wpair), full(sgc), full(sgb), full(wob),
            full(poolw), full(pscale), full(wo),
            pl.BlockSpec((None, n_hc, bt, d_conv), lambda i: (l, 0, si(i), 0)),
            pl.BlockSpec((None, n_hp, bt, d_pool), lambda i: (l, 0, si(i), 0)),
        ],
        out_specs=[
            xp_spec,
            pl.BlockSpec((None, SUBLANES, d_conv), lambda i: (pi(i) // per_seq, 0, 0)),
            pl.BlockSpec((None, MAX_WIN, d_pool), lambda i: (pi(i) // per_seq, 0, 0)),
            xs_spec,
            pl.BlockSpec((n_hc, bt, d_conv), lambda i: (0, si(i), 0)),
            pl.BlockSpec((n_hp, bt, d_pool), lambda i: (0, si(i), 0)),
            pl.BlockSpec((steps, bt, d_gmlp), lambda i: (0, si(i), 0)),
        ],
        out_shape=[
            jax.ShapeDtypeStruct(xp.shape, F32),
            jax.ShapeDtypeStruct((batch, SUBLANES, d_conv), F32),
            jax.ShapeDtypeStruct((batch, MAX_WIN, d_pool), F32),
            jax.ShapeDtypeStruct(xs.shape, F32),
            jax.ShapeDtypeStruct((n_hc, dec_batch, d_conv), F32),
            jax.ShapeDtypeStruct((n_hp, dec_batch, d_pool), F32),
            jax.ShapeDtypeStruct((steps, dec_batch, d_gmlp), F32),
        ],
        scratch_shapes=[
            pltpu.VMEM((d_conv // LANES, SUBLANES + tm, LANES), F32),
            pltpu.VMEM((d_pool // LANES, 2 * MAX_WIN + tm, LANES), F32),
            pltpu.VMEM((n_pool_tmp, 2 * MAX_WIN + tm, LANES), F32),
        ],
        compiler_params=pltpu.CompilerParams(
            dimension_semantics=("arbitrary",), vmem_limit_bytes=VMEM_LIMIT),
        name="mixer",
    )(xp, xs, mod_p, mod_all, ng, win, cw, woa, lng, wpair, sgc, sgb, wob, poolw, pscale, wo,
      hconv, hpool)


def kernel(x_prompt, x_sample, state_conv, state_pool, c_prompt, c_sample, norm_g, w_ada, b_ada,
           w1_gu, w1_dn, w2_gu, w2_dn, w_in, conv_w, w_out_a, ln_g, w_s, b_s, w_out_b,
           pool_w, pool_scale, w_o, final_norm_g):
    depth = w_in.shape[0]
    batch, seq, d = x_prompt.shape
    dec_batch, steps, _ = x_sample.shape
    d_gmlp = ln_g.shape[-1]
    head_dim = d_gmlp // G_HEADS
    assert seq % STEP_TOKENS == 0 and TOKEN_TILE % CHUNK == 0 and TOKEN_TILE % (steps * SUBLANES) == 0
    assert dec_batch % (STEP_TOKENS // steps) == 0 and steps <= CHUNK and 2 * head_dim == LANES

    c_all = jnp.concatenate([c_sample, c_prompt], axis=0)
    mod_all = _ada_call(c_all, w_ada, b_ada)
    mod_p = mod_all[:, :, dec_batch:, :].transpose(0, 2, 1, 3).reshape(depth, batch, N_SUB, 3, d)

    w1gu, w1dn = w1_gu.astype(BF16), w1_dn.astype(BF16)
    w2gu, w2dn = w2_gu.astype(BF16), w2_dn.astype(BF16)
    win = w_in.astype(BF16)
    woa = w_out_a.astype(BF16)
    wob = w_out_b.astype(BF16)
    wo = w_o.astype(BF16)
    poolw = pool_w.astype(BF16)
    ng = norm_g.reshape(depth, N_SUB, 1, d)
    lng = ln_g.reshape(depth, 1, d_gmlp)
    pscale = pool_scale.reshape(depth, 1, d)
    fg = final_norm_g.reshape(1, d)
    w_tril = w_s * jnp.tril(jnp.ones((CHUNK, CHUNK), w_s.dtype))
    wpair = w_tril.reshape(depth, G_HEADS // 2, 2, CHUNK, CHUNK).transpose(0, 1, 3, 2, 4)
    wpair = wpair.reshape(depth, G_HEADS // 2, CHUNK, 2 * CHUNK).astype(BF16)
    sgb = jnp.repeat(b_s.transpose(0, 2, 1), head_dim, axis=-1)
    sgc = jnp.repeat(w_tril[:, :, :steps, :steps].transpose(0, 2, 3, 1), head_dim, axis=-1)
    sgc = sgc.reshape(depth, steps * steps, d_gmlp)

    xp = x_prompt
    xs = x_sample.transpose(1, 0, 2)
    hconv = state_conv.transpose(0, 2, 1, 3)
    hpool = state_pool.transpose(0, 2, 1, 3)

    conv_p, conv_s, pool_p, pool_s, v_s = [], [], [], [], []
    for l in range(depth):
        last = l == depth - 1
        xp, xs = _ffn_call(xp, xs, mod_p, mod_all, ng, w1gu, w1dn, fg, l=l, sub=0, final_norm=False)
        xp, ctail, ptail, xs, ncs, nps, vs = _mixer_call(
            xp, xs, mod_p, mod_all, ng, win, conv_w, woa, lng, wpair, sgc, sgb, wob, poolw, pscale,
            wo, hconv, hpool, l=l)
        xp, xs = _ffn_call(xp, xs, mod_p, mod_all, ng, w2gu, w2dn, fg, l=l, sub=2, final_norm=last)
        conv_p.append(ctail[:, SUBLANES - (CONV_W - 1):])
        pool_p.append(ptail[:, 1:])
        conv_s.append(ncs.transpose(1, 0, 2))
        pool_s.append(nps.transpose(1, 0, 2))
        v_s.append(vs.transpose(1, 0, 2))

    return (xp, xs.transpose(1, 0, 2), jnp.stack(conv_p), jnp.stack(conv_s), jnp.stack(pool_p),
            jnp.stack(pool_s), jnp.stack(v_s))
```

```python
import functools
import math

import jax
import jax.numpy as jnp
from jax import lax
from jax.experimental import pallas as pl
from jax.experimental.pallas import tpu as pltpu

F32 = jnp.float32
BF16 = jnp.bfloat16

EPS = 1e-6
N_SUB = 3
CONV_W = 3
G_HEADS = 8
CHUNK = 128
POOL_WINDOWS = (2, 4, 8, 16)
MAX_WIN = max(POOL_WINDOWS)
PAST_LEN = 16384

LANES = 128
SUBLANES = 8
MXU_DIM = 256
TOKEN_TILE = 512
SUB_TILES = 1
STEP_TOKENS = TOKEN_TILE * SUB_TILES
FF_CHUNKS = 2
GU_STAGE_ROWS = 128
DN_STAGE_ROWS = 256
VMEM_LIMIT = 56 * 1024 * 1024


def _flat(a):
    return a.reshape(-1, a.shape[-1])


def _rms_mod(x, g, shift, scale):
    ms = jnp.mean(x * x, axis=-1, keepdims=True)
    y = x * lax.rsqrt(ms + EPS) * g
    return y * (1.0 + scale) + shift


_dot = functools.partial(jnp.dot, preferred_element_type=F32)


def _gelu(x):
    return 0.5 * x * (1.0 + lax.erf(x * (1.0 / math.sqrt(2.0))))


def _ada_kernel(c_ref, w_ref, b_ref, o_ref):
    c = c_ref[...]
    sc = (c * jax.nn.sigmoid(c)).astype(BF16)
    o_ref[...] = _dot(sc, w_ref[...].astype(BF16)) + b_ref[...]


def _ada_call(c_all, w_ada, b_ada):
    depth, d, cols = w_ada.shape
    n_mod = cols // d
    rows = c_all.shape[0]
    return pl.pallas_call(
        _ada_kernel,
        grid=(depth, n_mod),
        in_specs=[
            pl.BlockSpec((rows, d), lambda l, j: (0, 0)),
            pl.BlockSpec((None, d, d), lambda l, j: (l, 0, j)),
            pl.BlockSpec((None, None, 1, d), lambda l, j: (l, j, 0, 0)),
        ],
        out_specs=pl.BlockSpec((None, None, rows, d), lambda l, j: (l, j, 0, 0)),
        out_shape=jax.ShapeDtypeStruct((depth, n_mod, rows, d), F32),
        compiler_params=pltpu.CompilerParams(
            dimension_semantics=("arbitrary", "arbitrary"), vmem_limit_bytes=VMEM_LIMIT),
        name="ada_mod",
    )(c_all, w_ada, b_ada.reshape(depth, n_mod, 1, d))


def _ff_bounds(d_ff):
    tiles = d_ff // MXU_DIM
    per = -(-tiles // FF_CHUNKS)
    return [min(i * per, tiles) * MXU_DIM for i in range(FF_CHUNKS + 1)]


def _rows(ref, i, n):
    start = pl.multiple_of(i * n, n)
    return ref.at[(slice(None),) * (len(ref.shape) - 2) + (pl.ds(start, n),)]


def _ffn_kernel(x_ref, mod_ref, ng_ref, wgu_ref, wdn_ref, fg_ref, o_ref, *, final_norm, per_row_mod):
    n_sub = SUB_TILES
    rows = x_ref.shape[-2] // n_sub
    d_ff = wdn_ref.shape[0]
    bounds = _ff_bounds(d_ff)

    def tile(i, carry):
        x = _rows(x_ref, i, rows)[...]
        mod = _rows(mod_ref, i, rows) if per_row_mod else mod_ref
        h = _rms_mod(x, ng_ref[...], mod[0:1], mod[1:2])
        hb = _flat(h).astype(BF16)
        acc = None
        for c0, c1 in zip(bounds[:-1], bounds[1:]):
            a = _dot(hb, wgu_ref[:, c0:c1])
            b = _dot(hb, wgu_ref[:, d_ff + c0:d_ff + c1])
            g = (a * jax.nn.sigmoid(a)) * b
            part = _dot(g.astype(BF16), wdn_ref[c0:c1, :])
            acc = part if acc is None else acc + part
        y = x + (0.5 * mod[2:3]) * acc.reshape(x.shape)
        if final_norm:
            ms = jnp.mean(y * y, axis=-1, keepdims=True)
            y = y * lax.rsqrt(ms + EPS) * fg_ref[...]
        _rows(o_ref, i, rows)[...] = y
        return carry

    lax.fori_loop(0, n_sub, tile, 0)


def _const_spec(shape, index_map):
    return pl.BlockSpec(shape, index_map, pipeline_mode=pl.Buffered(1))


def _load_bf16(src, dst, stage, sems):
    rpc = stage.shape[1]
    n = src.shape[0] // rpc

    def copy(c):
        return pltpu.make_async_copy(src.at[pl.ds(c * rpc, rpc), :], stage.at[c % 2], sems.at[c % 2])

    copy(0).start()
    for c in range(n):
        if c + 1 < n:
            copy(c + 1).start()
        copy(c).wait()
        dst[c * rpc:(c + 1) * rpc, :] = stage[c % 2].astype(BF16)


def _ffn_pair_kernel(xp_ref, xs_ref, modp_ref, mods_ref, ng_ref, wgu_hbm, wdn_hbm, fg_ref,
                     op_ref, os_ref, wgu_ref, wdn_ref, gu_stage, dn_stage, sems,
                     *, final_norm, prompt_steps, layer):
    i = pl.program_id(0)

    @pl.when(i == 0)
    def _():
        _load_bf16(wgu_hbm.at[layer], wgu_ref, gu_stage, sems.at[0])
        _load_bf16(wdn_hbm.at[layer], wdn_ref, dn_stage, sems.at[1])

    @pl.when(i < prompt_steps)
    def _():
        _ffn_kernel(xp_ref, modp_ref, ng_ref, wgu_ref, wdn_ref, fg_ref, op_ref,
                    final_norm=final_norm, per_row_mod=False)

    @pl.when(i >= prompt_steps)
    def _():
        _ffn_kernel(xs_ref, mods_ref, ng_ref, wgu_ref, wdn_ref, fg_ref, os_ref,
                    final_norm=final_norm, per_row_mod=True)


def _ffn_call(xp, xs, mod_p, mod_all, ng, wgu, wdn, fg, *, l, sub, final_norm):
    batch, seq, d = xp.shape
    steps, dec_batch, _ = xs.shape
    two_ff = wgu.shape[-1]
    per_seq = seq // STEP_TOKENS
    prompt_steps = batch * per_seq
    bt = STEP_TOKENS // steps
    pi = lambda i: jnp.minimum(i, prompt_steps - 1)
    si = lambda i: jnp.maximum(i - prompt_steps, 0)
    xp_spec = pl.BlockSpec((None, STEP_TOKENS, d), lambda i: (pi(i) // per_seq, pi(i) % per_seq, 0))
    xs_spec = pl.BlockSpec((steps, bt, d), lambda i: (0, si(i), 0))
    return pl.pallas_call(
        functools.partial(_ffn_pair_kernel, final_norm=final_norm, prompt_steps=prompt_steps,
                          layer=l),
        grid=(prompt_steps + dec_batch // bt,),
        in_specs=[
            xp_spec,
            xs_spec,
            pl.BlockSpec((None, None, None, N_SUB, d), lambda i: (l, pi(i) // per_seq, sub, 0, 0)),
            pl.BlockSpec((None, N_SUB, bt, d), lambda i: (l, sub, si(i), 0)),
            pl.BlockSpec((None, None, 1, d), lambda i: (l, sub, 0, 0)),
            pl.BlockSpec(memory_space=pl.ANY),
            pl.BlockSpec(memory_space=pl.ANY),
            pl.BlockSpec((1, d), lambda i: (0, 0)),
        ],
        out_specs=[xp_spec, xs_spec],
        out_shape=[jax.ShapeDtypeStruct(xp.shape, F32), jax.ShapeDtypeStruct(xs.shape, F32)],
        scratch_shapes=[
            pltpu.VMEM((d, two_ff), BF16),
            pltpu.VMEM((two_ff // 2, d), BF16),
            pltpu.VMEM((2, GU_STAGE_ROWS, two_ff), F32),
            pltpu.VMEM((2, DN_STAGE_ROWS, d), F32),
            pltpu.SemaphoreType.DMA((2, 2)),
        ],
        compiler_params=pltpu.CompilerParams(
            dimension_semantics=("arbitrary",), vmem_limit_bytes=VMEM_LIMIT),
        name="ffn",
    )(xp, xs, mod_p, mod_all, ng, wgu, wdn, fg)


def _mixer_core(x, mod_ref, ng_ref, win_ref, woa_ref, lng_ref, wob_ref, poolw_ref, pscale_ref,
                wo_ref, conv_fn, gate_fn, pool_fn):
    d_conv = woa_ref.shape[0]
    d_gmlp = wob_ref.shape[0]
    d_pool = poolw_ref.shape[0] * poolw_ref.shape[1]
    d_model = wo_ref.shape[0]
    h = _rms_mod(x, ng_ref[...], mod_ref[0:1], mod_ref[1:2])
    hb = _flat(h).astype(BF16)

    win = lambda lo, n: win_ref[:, lo:lo + n]
    o_b = 3 * d_conv
    o_c = o_b + 2 * d_gmlp
    o_g = o_c + d_pool

    pa = _dot(hb, win(0, 3 * d_conv))
    xa = pa[:, :d_conv]
    bg = pa[:, d_conv:2 * d_conv]
    cg = pa[:, 2 * d_conv:]
    conv = conv_fn(cg * xa)
    a_in = (bg * conv).astype(BF16)

    puv = _dot(hb, win(o_b, 2 * d_gmlp))
    y_a = _dot(a_in, woa_ref[...])
    u = _gelu(puv[:, :d_gmlp])
    v = _gelu(puv[:, d_gmlp:])
    mu = jnp.mean(v, axis=-1, keepdims=True)
    vc = v - mu
    var = jnp.mean(vc * vc, axis=-1, keepdims=True)
    v = vc * lax.rsqrt(var + EPS) * lng_ref[...]

    p = _dot(hb, win(o_c, d_pool))
    merged = jax.nn.sigmoid(_dot(hb, win(o_g, d_model))) * y_a
    sg = gate_fn(v)
    gate_b = _dot(hb, win(o_g + d_model, d_model))

    diff = (pool_fn(p) - p).astype(BF16)
    gdim = poolw_ref.shape[1]
    y_b = _dot((u * sg).astype(BF16), wob_ref[...])
    y_c = jnp.concatenate(
        [_dot(diff[:, g * gdim:(g + 1) * gdim], poolw_ref[g]) for g in range(poolw_ref.shape[0])],
        axis=1) * pscale_ref[...]
    gate_c = _dot(hb, win(o_g + 2 * d_model, d_model))
    merged = merged + jax.nn.sigmoid(gate_b) * y_b + jax.nn.sigmoid(gate_c) * y_c
    m = _dot(merged.astype(BF16), wo_ref[...])
    return x + mod_ref[2:3] * m.reshape(x.shape)


def _mixer_prompt_kernel(x_ref, mod_ref, ng_ref, win_ref, cw_ref, woa_ref, lng_ref, wpair_ref,
                         sgb_ref, wob_ref, poolw_ref, pscale_ref, wo_ref,
                         o_ref, ctail_ref, ptail_ref, zc_ref, pc_ref, pd_ref, *, seq_step):
    tm = x_ref.shape[0] // SUB_TILES
    zh = zc_ref.shape[1] - tm
    ph = pc_ref.shape[1] - tm
    lax.fori_loop(0, SUB_TILES, functools.partial(
        _mixer_prompt_tile, x_ref, mod_ref, ng_ref, win_ref, cw_ref, woa_ref, lng_ref, wpair_ref,
        sgb_ref, wob_ref, poolw_ref, pscale_ref, wo_ref, o_ref, ctail_ref, ptail_ref, zc_ref,
        pc_ref, pd_ref, tm, zh, ph, seq_step), 0)


def _mixer_prompt_tile(x_ref, mod_ref, ng_ref, win_ref, cw_ref, woa_ref, lng_ref, wpair_ref,
                       sgb_ref, wob_ref, poolw_ref, pscale_ref, wo_ref, o_ref, ctail_ref,
                       ptail_ref, zc_ref, pc_ref, pd_ref, tm, zh, ph, seq_step, i, carry):
    t = seq_step * SUB_TILES + i

    @pl.when(t == 0)
    def _():
        zc_ref[:, 0:zh, :] = jnp.zeros((zc_ref.shape[0], zh, LANES), F32)
        pc_ref[:, 0:ph, :] = jnp.zeros((pc_ref.shape[0], ph, LANES), F32)

    def conv_fn(z):
        outs = []
        for g in range(zc_ref.shape[0]):
            ln = slice(g * LANES, (g + 1) * LANES)
            zg = z[:, ln]
            zc_ref[g, zh:, :] = zg
            acc = zg * cw_ref[CONV_W - 1:CONV_W, ln]
            for k in range(1, CONV_W):
                acc = acc + zc_ref[g, zh - k:zh - k + tm, :] * cw_ref[CONV_W - 1 - k:CONV_W - k, ln]
            tail = zg[tm - SUBLANES:]
            zc_ref[g, zh - SUBLANES:zh, :] = tail
            ctail_ref[:, ln] = tail
            outs.append(acc)
        return jnp.concatenate(outs, axis=1)

    def gate_fn(v):
        lane = lax.broadcasted_iota(jnp.int32, (CHUNK, LANES), 1)
        low = lane < (LANES // 2)
        n_blk = v.shape[1] // LANES
        rows = []
        for c in range(tm // CHUNK):
            vc = v[c * CHUNK:(c + 1) * CHUNK]
            outs = []
            for j in range(n_blk):
                blk = vc[:, j * LANES:(j + 1) * LANES]
                rhs = jnp.concatenate(
                    [jnp.where(low, blk, 0.0), jnp.where(low, 0.0, blk)], axis=0).astype(BF16)
                outs.append(_dot(wpair_ref[j], rhs))
            rows.append(jnp.concatenate(outs, axis=1) + sgb_ref[...])
        return jnp.concatenate(rows, axis=0)

    def pool_fn(p):
        pos = t * tm + lax.broadcasted_iota(jnp.int32, (MAX_WIN, 1), 0)
        outs = []
        buf = 0
        for gi, w in enumerate(POOL_WINDOWS):
            ln = slice(gi * LANES, (gi + 1) * LANES)
            pg = p[:, ln]
            pc_ref[gi, ph:, :] = pg
            src = pc_ref.at[gi]
            k = 1
            lo = 0
            while 2 * k < w:
                lo += SUBLANES
                dst = pd_ref.at[buf]
                buf += 1
                dst[lo:, :] = src[lo:, :] + src[lo - k:ph + tm - k, :]
                src = dst
                k *= 2
            s = src[ph:, :] + src[ph - k:ph + tm - k, :]
            head = s[:MAX_WIN] * (1.0 / jnp.minimum(pos + 1, w).astype(F32))
            outs.append(jnp.concatenate([head, s[MAX_WIN:] * (1.0 / w)], axis=0))
            tail = pg[tm - MAX_WIN:]
            pc_ref[gi, ph - MAX_WIN:ph, :] = tail
            ptail_ref[:, ln] = tail
        return jnp.concatenate(outs, axis=1)

    _rows(o_ref, i, tm)[...] = _mixer_core(
        _rows(x_ref, i, tm)[...], mod_ref, ng_ref, win_ref, woa_ref, lng_ref, wob_ref,
        poolw_ref, pscale_ref, wo_ref, conv_fn, gate_fn, pool_fn)
    return carry


def _mixer_sample_kernel(x_ref, mod_ref, ng_ref, win_ref, cw_ref, woa_ref, lng_ref, sgc_ref,
                         sgb_ref, wob_ref, poolw_ref, pscale_ref, wo_ref, hc_ref, hp_ref,
                         o_ref, nconv_ref, npool_ref, vopen_ref):
    bt = x_ref.shape[1] // SUB_TILES
    lax.fori_loop(0, SUB_TILES, functools.partial(
        _mixer_sample_tile, x_ref, mod_ref, ng_ref, win_ref, cw_ref, woa_ref, lng_ref, sgc_ref,
        sgb_ref, wob_ref, poolw_ref, pscale_ref, wo_ref, hc_ref, hp_ref, o_ref, nconv_ref,
        npool_ref, vopen_ref, bt), 0)


def _mixer_sample_tile(x_ref, mod_ref, ng_ref, win_ref, cw_ref, woa_ref, lng_ref, sgc_ref,
                       sgb_ref, wob_ref, poolw_ref, pscale_ref, wo_ref, hc_ref, hp_ref, o_ref,
                       nconv_ref, npool_ref, vopen_ref, bt, i, carry):
    x_ref, mod_ref, hc_ref, hp_ref, o_ref, nconv_ref, npool_ref, vopen_ref = (
        _rows(r, i, bt) for r in (x_ref, mod_ref, hc_ref, hp_ref, o_ref, nconv_ref, npool_ref,
                                  vopen_ref))
    steps = x_ref.shape[0]

    def conv_fn(z):
        z3 = z.reshape(steps, bt, z.shape[-1])
        ext = jnp.concatenate([hc_ref[...], z3], axis=0)
        nconv_ref[...] = ext[steps:]
        conv = sum(ext[k:k + steps] * cw_ref[k:k + 1] for k in range(CONV_W))
        return _flat(conv)

    def gate_fn(v):
        v3 = v.reshape(steps, bt, v.shape[-1])
        vopen_ref[...] = v3
        outs = []
        for tt in range(steps):
            acc = sgb_ref[tt:tt + 1] + sgc_ref[tt * steps:tt * steps + 1] * v3[0]
            for s in range(1, tt + 1):
                acc = acc + sgc_ref[tt * steps + s:tt * steps + s + 1] * v3[s]
            outs.append(acc)
        return _flat(jnp.stack(outs, axis=0))

    def pool_fn(p):
        p3 = p.reshape(steps, bt, p.shape[-1])
        ext = jnp.concatenate([hp_ref[...], p3], axis=0)
        hist = hp_ref.shape[0]
        npool_ref[...] = ext[steps:]
        outs = []
        for gi, w in enumerate(POOL_WINDOWS):
            s = ext[:, :, gi * LANES:(gi + 1) * LANES]
            k = 1
            while k < w:
                s = s[k:] + s[:-k]
                k *= 2
            first = hist - (w - 1)
            outs.append(jnp.stack(
                [s[first + tt] * (1.0 / min(PAST_LEN + tt + 1, w)) for tt in range(steps)], axis=0))
        return _flat(jnp.concatenate(outs, axis=-1))

    o_ref[...] = _mixer_core(x_ref[...], mod_ref, ng_ref, win_ref, woa_ref, lng_ref, wob_ref,
                             poolw_ref, pscale_ref, wo_ref, conv_fn, gate_fn, pool_fn)
    return carry


def _mixer_pair_kernel(xp_ref, xs_ref, modp_ref, mods_ref, ng_ref, win_ref, cw_ref, woa_ref, lng_ref,
                       wpair_ref, sgc_ref, sgb_ref, wob_ref, poolw_ref, pscale_ref, wo_ref,
                       hc_ref, hp_ref, op_ref, ctail_ref, ptail_ref, os_ref, nconv_ref, npool_ref,
                       vopen_ref, zc_ref, pc_ref, pd_ref, *, prompt_steps, per_seq):
    i = pl.program_id(0)

    @pl.when(i < prompt_steps)
    def _():
        _mixer_prompt_kernel(xp_ref, modp_ref, ng_ref, win_ref, cw_ref, woa_ref, lng_ref, wpair_ref,
                             sgb_ref, wob_ref, poolw_ref, pscale_ref, wo_ref, op_ref, ctail_ref,
                             ptail_ref, zc_ref, pc_ref, pd_ref, seq_step=i % per_seq)

    @pl.when(i >= prompt_steps)
    def _():
        _mixer_sample_kernel(xs_ref, mods_ref, ng_ref, win_ref, cw_ref, woa_ref, lng_ref, sgc_ref,
                             sgb_ref, wob_ref, poolw_ref, pscale_ref, wo_ref, hc_ref, hp_ref,
                             os_ref, nconv_ref, npool_ref, vopen_ref)


def _mixer_call(xp, xs, mod_p, mod_all, ng, win, cw, woa, lng, wpair, sgc, sgb, wob, poolw, pscale,
                wo, hconv, hpool, *, l):
    batch, seq, d = xp.shape
    steps, dec_batch, _ = xs.shape
    d_conv = cw.shape[-1]
    d_gmlp = wob.shape[1]
    d_pool = poolw.shape[2] * poolw.shape[1]
    sub = 1
    tm = TOKEN_TILE
    per_seq = seq // STEP_TOKENS
    prompt_steps = batch * per_seq
    bt = STEP_TOKENS // steps
    n_hc, n_hp = hconv.shape[1], hpool.shape[1]
    n_pool_tmp = sum(max(w.bit_length() - 2, 0) for w in POOL_WINDOWS)
    pi = lambda i: jnp.minimum(i, prompt_steps - 1)
    si = lambda i: jnp.maximum(i - prompt_steps, 0)
    full = lambda a: _const_spec((None,) + a.shape[1:], lambda i: (l,) + (0,) * (a.ndim - 1))
    xp_spec = pl.BlockSpec((None, STEP_TOKENS, d), lambda i: (pi(i) // per_seq, pi(i) % per_seq, 0))
    xs_spec = pl.BlockSpec((steps, bt, d), lambda i: (0, si(i), 0))
    return pl.pallas_call(
        functools.partial(_mixer_pair_kernel, prompt_steps=prompt_steps, per_seq=per_seq),
        grid=(prompt_steps + dec_batch // bt,),
        in_specs=[
            xp_spec,
            xs_spec,
            pl.BlockSpec((None, None, None, N_SUB, d), lambda i: (l, pi(i) // per_seq, sub, 0, 0)),
            pl.BlockSpec((None, N_SUB, bt, d), lambda i: (l, sub, si(i), 0)),
            pl.BlockSpec((None, None, 1, d), lambda i: (l, sub, 0, 0)),
            full(win), full(cw), full(woa), full(lng), full(wpair), full(sgc), full(sgb), full(wob),
            full(poolw), full(pscale), full(wo),
            pl.BlockSpec((None, n_hc, bt, d_conv), lambda i: (l, 0, si(i), 0)),
            pl.BlockSpec((None, n_hp, bt, d_pool), lambda i: (l, 0, si(i), 0)),
        ],
        out_specs=[
            xp_spec,
            pl.BlockSpec((None, SUBLANES, d_conv), lambda i: (pi(i) // per_seq, 0, 0)),
            pl.BlockSpec((None, MAX_WIN, d_pool), lambda i: (pi(i) // per_seq, 0, 0)),
            xs_spec,
            pl.BlockSpec((n_hc, bt, d_conv), lambda i: (0, si(i), 0)),
            pl.BlockSpec((n_hp, bt, d_pool), lambda i: (0, si(i), 0)),
            pl.BlockSpec((steps, bt, d_gmlp), lambda i: (0, si(i), 0)),
        ],
        out_shape=[
            jax.ShapeDtypeStruct(xp.shape, F32),
            jax.ShapeDtypeStruct((batch, SUBLANES, d_conv), F32),
            jax.ShapeDtypeStruct((batch, MAX_WIN, d_pool), F32),
            jax.ShapeDtypeStruct(xs.shape, F32),
            jax.ShapeDtypeStruct((n_hc, dec_batch, d_conv), F32),
            jax.ShapeDtypeStruct((n_hp, dec_batch, d_pool), F32),
            jax.ShapeDtypeStruct((steps, dec_batch, d_gmlp), F32),
        ],
        scratch_shapes=[
            pltpu.VMEM((d_conv // LANES, SUBLANES + tm, LANES), F32),
            pltpu.VMEM((d_pool // LANES, 2 * MAX_WIN + tm, LANES), F32),
            pltpu.VMEM((n_pool_tmp, 2 * MAX_WIN + tm, LANES), F32),
        ],
        compiler_params=pltpu.CompilerParams(
            dimension_semantics=("arbitrary",), vmem_limit_bytes=VMEM_LIMIT),
        name="mixer",
    )(xp, xs, mod_p, mod_all, ng, win, cw, woa, lng, wpair, sgc, sgb, wob, poolw, pscale, wo,
      hconv, hpool)


def kernel(x_prompt, x_sample, state_conv, state_pool, c_prompt, c_sample, norm_g, w_ada, b_ada,
           w1_gu, w1_dn, w2_gu, w2_dn, w_in, conv_w, w_out_a, ln_g, w_s, b_s, w_out_b,
           pool_w, pool_scale, w_o, final_norm_g):
    depth = w_in.shape[0]
    batch, seq, d = x_prompt.shape
    dec_batch, steps, _ = x_sample.shape
    d_gmlp = ln_g.shape[-1]
    head_dim = d_gmlp // G_HEADS
    assert seq % STEP_TOKENS == 0 and TOKEN_TILE % CHUNK == 0 and TOKEN_TILE % (steps * SUBLANES) == 0
    assert dec_batch % (STEP_TOKENS // steps) == 0 and steps <= CHUNK and 2 * head_dim == LANES

    c_all = jnp.concatenate([c_sample, c_prompt], axis=0)
    mod_all = _ada_call(c_all, w_ada, b_ada)
    mod_p = mod_all[:, :, dec_batch:, :].transpose(0, 2, 1, 3).reshape(depth, batch, N_SUB, 3, d)

    w1gu, w1dn, w2gu, w2dn = w1_gu, w1_dn, w2_gu, w2_dn
    assert w1_gu.shape[1] % GU_STAGE_ROWS == 0 and w1_dn.shape[1] % DN_STAGE_ROWS == 0
    win = w_in.astype(BF16)
    woa = w_out_a.astype(BF16)
    wob = w_out_b.astype(BF16)
    wo = w_o.astype(BF16)
    poolw = pool_w.astype(BF16)
    ng = norm_g.reshape(depth, N_SUB, 1, d)
    lng = ln_g.reshape(depth, 1, d_gmlp)
    pscale = pool_scale.reshape(depth, 1, d)
    fg = final_norm_g.reshape(1, d)
    w_tril = w_s * jnp.tril(jnp.ones((CHUNK, CHUNK), w_s.dtype))
    wpair = w_tril.reshape(depth, G_HEADS // 2, 2, CHUNK, CHUNK).transpose(0, 1, 3, 2, 4)
    wpair = wpair.reshape(depth, G_HEADS // 2, CHUNK, 2 * CHUNK).astype(BF16)
    sgb = jnp.repeat(b_s.transpose(0, 2, 1), head_dim, axis=-1)
    sgc = jnp.repeat(w_tril[:, :, :steps, :steps].transpose(0, 2, 3, 1), head_dim, axis=-1)
    sgc = sgc.reshape(depth, steps * steps, d_gmlp)

    xp = x_prompt
    xs = x_sample.transpose(1, 0, 2)
    hconv = state_conv.transpose(0, 2, 1, 3)
    hpool = state_pool.transpose(0, 2, 1, 3)

    conv_p, conv_s, pool_p, pool_s, v_s = [], [], [], [], []
    for l in range(depth):
        last = l == depth - 1
        xp, xs = _ffn_call(xp, xs, mod_p, mod_all, ng, w1gu, w1dn, fg, l=l, sub=0, final_norm=False)
        xp, ctail, ptail, xs, ncs, nps, vs = _mixer_call(
            xp, xs, mod_p, mod_all, ng, win, conv_w, woa, lng, wpair, sgc, sgb, wob, poolw, pscale,
            wo, hconv, hpool, l=l)
        xp, xs = _ffn_call(xp, xs, mod_p, mod_all, ng, w2gu, w2dn, fg, l=l, sub=2, final_norm=last)
        conv_p.append(ctail[:, SUBLANES - (CONV_W - 1):])
        pool_p.append(ptail[:, 1:])
        conv_s.append(ncs.transpose(1, 0, 2))
        pool_s.append(nps.transpose(1, 0, 2))
        v_s.append(vs.transpose(1, 0, 2))

    return (xp, xs.transpose(1, 0, 2), jnp.stack(conv_p), jnp.stack(conv_s), jnp.stack(pool_p),
            jnp.stack(pool_s), jnp.stack(v_s))
```

```python
import functools
import math

import jax
import jax.numpy as jnp
from jax import lax
from jax.experimental import pallas as pl
from jax.experimental.pallas import tpu as pltpu

F32 = jnp.float32
BF16 = jnp.bfloat16

EPS = 1e-6
N_SUB = 3
CONV_W = 3
G_HEADS = 8
CHUNK = 128
POOL_WINDOWS = (2, 4, 8, 16)
MAX_WIN = max(POOL_WINDOWS)
PAST_LEN = 16384

LANES = 128
SUBLANES = 8
MXU_DIM = 256
TOKEN_TILE = 512
SUB_TILES = 1
STEP_TOKENS = TOKEN_TILE * SUB_TILES
FF_CHUNKS = 2
GU_STAGE_ROWS = 128
DN_STAGE_ROWS = 256
STAGE_SLOTS = 3
VMEM_LIMIT = 56 * 1024 * 1024


def _flat(a):
    return a.reshape(-1, a.shape[-1])


def _rms_mod(x, g, shift, scale):
    ms = jnp.mean(x * x, axis=-1, keepdims=True)
    y = x * lax.rsqrt(ms + EPS) * g
    return y * (1.0 + scale) + shift


_dot = functools.partial(jnp.dot, preferred_element_type=F32)


def _gelu(x):
    return 0.5 * x * (1.0 + lax.erf(x * (1.0 / math.sqrt(2.0))))


def _ada_kernel(c_ref, w_ref, b_ref, o_ref):
    c = c_ref[...]
    sc = (c * jax.nn.sigmoid(c)).astype(BF16)
    o_ref[...] = _dot(sc, w_ref[...].astype(BF16)) + b_ref[...]


def _ada_call(c_all, w_ada, b_ada):
    depth, d, cols = w_ada.shape
    n_mod = cols // d
    rows = c_all.shape[0]
    return pl.pallas_call(
        _ada_kernel,
        grid=(depth, n_mod),
        in_specs=[
            pl.BlockSpec((rows, d), lambda l, j: (0, 0)),
            pl.BlockSpec((None, d, d), lambda l, j: (l, 0, j)),
            pl.BlockSpec((None, None, 1, d), lambda l, j: (l, j, 0, 0)),
        ],
        out_specs=pl.BlockSpec((None, None, rows, d), lambda l, j: (l, j, 0, 0)),
        out_shape=jax.ShapeDtypeStruct((depth, n_mod, rows, d), F32),
        compiler_params=pltpu.CompilerParams(
            dimension_semantics=("arbitrary", "arbitrary"), vmem_limit_bytes=VMEM_LIMIT),
        name="ada_mod",
    )(c_all, w_ada, b_ada.reshape(depth, n_mod, 1, d))


def _ff_bounds(d_ff):
    tiles = d_ff // MXU_DIM
    per = -(-tiles // FF_CHUNKS)
    return [min(i * per, tiles) * MXU_DIM for i in range(FF_CHUNKS + 1)]


def _rows(ref, i, n):
    start = pl.multiple_of(i * n, n)
    return ref.at[(slice(None),) * (len(ref.shape) - 2) + (pl.ds(start, n),)]


def _ffn_kernel(x_ref, mod_ref, ng_ref, wgu_ref, wdn_ref, fg_ref, o_ref, *, final_norm, per_row_mod):
    n_sub = SUB_TILES
    rows = x_ref.shape[-2] // n_sub
    d_ff = wdn_ref.shape[0]
    bounds = _ff_bounds(d_ff)

    def tile(i, carry):
        x = _rows(x_ref, i, rows)[...]
        mod = _rows(mod_ref, i, rows) if per_row_mod else mod_ref
        h = _rms_mod(x, ng_ref[...], mod[0:1], mod[1:2])
        hb = _flat(h).astype(BF16)
        acc = None
        for c0, c1 in zip(bounds[:-1], bounds[1:]):
            a = _dot(hb, wgu_ref[:, c0:c1])
            b = _dot(hb, wgu_ref[:, d_ff + c0:d_ff + c1])
            g = (a * jax.nn.sigmoid(a)) * b
            part = _dot(g.astype(BF16), wdn_ref[c0:c1, :])
            acc = part if acc is None else acc + part
        y = x + (0.5 * mod[2:3]) * acc.reshape(x.shape)
        if final_norm:
            ms = jnp.mean(y * y, axis=-1, keepdims=True)
            y = y * lax.rsqrt(ms + EPS) * fg_ref[...]
        _rows(o_ref, i, rows)[...] = y
        return carry

    lax.fori_loop(0, n_sub, tile, 0)


def _const_spec(shape, index_map):
    return pl.BlockSpec(shape, index_map, pipeline_mode=pl.Buffered(1))


def _load_bf16(src, dst, stage, sems):
    slots, rpc = stage.shape[0], stage.shape[1]
    n = src.shape[0] // rpc

    def copy(c):
        slot = c % slots
        return pltpu.make_async_copy(src.at[pl.ds(c * rpc, rpc), :], stage.at[slot], sems.at[slot])

    for c in range(min(slots - 1, n)):
        copy(c).start()
    for c in range(n):
        if c + slots - 1 < n:
            copy(c + slots - 1).start()
        copy(c).wait()
        dst[c * rpc:(c + 1) * rpc, :] = stage[c % slots].astype(BF16)


def _ffn_pair_kernel(xp_ref, xs_ref, modp_ref, mods_ref, ng_ref, wgu_hbm, wdn_hbm, fg_ref,
                     op_ref, os_ref, wgu_ref, wdn_ref, gu_stage, dn_stage, sems,
                     *, final_norm, prompt_steps, layer):
    i = pl.program_id(0)

    @pl.when(i == 0)
    def _():
        _load_bf16(wgu_hbm.at[layer], wgu_ref, gu_stage, sems.at[0])
        _load_bf16(wdn_hbm.at[layer], wdn_ref, dn_stage, sems.at[1])

    @pl.when(i < prompt_steps)
    def _():
        _ffn_kernel(xp_ref, modp_ref, ng_ref, wgu_ref, wdn_ref, fg_ref, op_ref,
                    final_norm=final_norm, per_row_mod=False)

    @pl.when(i >= prompt_steps)
    def _():
        _ffn_kernel(xs_ref, mods_ref, ng_ref, wgu_ref, wdn_ref, fg_ref, os_ref,
                    final_norm=final_norm, per_row_mod=True)


def _ffn_call(xp, xs, mod_p, mod_all, ng, wgu, wdn, fg, *, l, sub, final_norm):
    batch, seq, d = xp.shape
    steps, dec_batch, _ = xs.shape
    two_ff = wgu.shape[-1]
    per_seq = seq // STEP_TOKENS
    prompt_steps = batch * per_seq
    bt = STEP_TOKENS // steps
    pi = lambda i: jnp.minimum(i, prompt_steps - 1)
    si = lambda i: jnp.maximum(i - prompt_steps, 0)
    xp_spec = pl.BlockSpec((None, STEP_TOKENS, d), lambda i: (pi(i) // per_seq, pi(i) % per_seq, 0))
    xs_spec = pl.BlockSpec((steps, bt, d), lambda i: (0, si(i), 0))
    return pl.pallas_call(
        functools.partial(_ffn_pair_kernel, final_norm=final_norm, prompt_steps=prompt_steps,
                          layer=l),
        grid=(prompt_steps + dec_batch // bt,),
        in_specs=[
            xp_spec,
            xs_spec,
            pl.BlockSpec((None, None, None, N_SUB, d), lambda i: (l, pi(i) // per_seq, sub, 0, 0)),
            pl.BlockSpec((None, N_SUB, bt, d), lambda i: (l, sub, si(i), 0)),
            pl.BlockSpec((None, None, 1, d), lambda i: (l, sub, 0, 0)),
            pl.BlockSpec(memory_space=pl.ANY),
            pl.BlockSpec(memory_space=pl.ANY),
            pl.BlockSpec((1, d), lambda i: (0, 0)),
        ],
        out_specs=[xp_spec, xs_spec],
        out_shape=[jax.ShapeDtypeStruct(xp.shape, F32), jax.ShapeDtypeStruct(xs.shape, F32)],
        scratch_shapes=[
            pltpu.VMEM((d, two_ff), BF16),
            pltpu.VMEM((two_ff // 2, d), BF16),
            pltpu.VMEM((STAGE_SLOTS, GU_STAGE_ROWS, two_ff), F32),
            pltpu.VMEM((STAGE_SLOTS, DN_STAGE_ROWS, d), F32),
            pltpu.SemaphoreType.DMA((2, STAGE_SLOTS)),
        ],
        compiler_params=pltpu.CompilerParams(
            dimension_semantics=("arbitrary",), vmem_limit_bytes=VMEM_LIMIT),
        name="ffn",
    )(xp, xs, mod_p, mod_all, ng, wgu, wdn, fg)


def _mixer_core(x, mod_ref, ng_ref, win_ref, woa_ref, lng_ref, wob_ref, poolw_ref, pscale_ref,
                wo_ref, conv_fn, gate_fn, pool_fn):
    d_conv = woa_ref.shape[0]
    d_gmlp = wob_ref.shape[0]
    d_pool = poolw_ref.shape[0] * poolw_ref.shape[1]
    d_model = wo_ref.shape[0]
    h = _rms_mod(x, ng_ref[...], mod_ref[0:1], mod_ref[1:2])
    hb = _flat(h).astype(BF16)

    win = lambda lo, n: win_ref[:, lo:lo + n]
    o_b = 3 * d_conv
    o_c = o_b + 2 * d_gmlp
    o_g = o_c + d_pool

    pa = _dot(hb, win(0, 3 * d_conv))
    xa = pa[:, :d_conv]
    bg = pa[:, d_conv:2 * d_conv]
    cg = pa[:, 2 * d_conv:]
    conv = conv_fn(cg * xa)
    a_in = (bg * conv).astype(BF16)

    puv = _dot(hb, win(o_b, 2 * d_gmlp))
    y_a = _dot(a_in, woa_ref[...])
    u = _gelu(puv[:, :d_gmlp])
    v = _gelu(puv[:, d_gmlp:])
    mu = jnp.mean(v, axis=-1, keepdims=True)
    vc = v - mu
    var = jnp.mean(vc * vc, axis=-1, keepdims=True)
    v = vc * lax.rsqrt(var + EPS) * lng_ref[...]

    p = _dot(hb, win(o_c, d_pool))
    merged = jax.nn.sigmoid(_dot(hb, win(o_g, d_model))) * y_a
    sg = gate_fn(v)
    gate_b = _dot(hb, win(o_g + d_model, d_model))

    diff = (pool_fn(p) - p).astype(BF16)
    gdim = poolw_ref.shape[1]
    y_b = _dot((u * sg).astype(BF16), wob_ref[...])
    y_c = jnp.concatenate(
        [_dot(diff[:, g * gdim:(g + 1) * gdim], poolw_ref[g]) for g in range(poolw_ref.shape[0])],
        axis=1) * pscale_ref[...]
    gate_c = _dot(hb, win(o_g + 2 * d_model, d_model))
    merged = merged + jax.nn.sigmoid(gate_b) * y_b + jax.nn.sigmoid(gate_c) * y_c
    m = _dot(merged.astype(BF16), wo_ref[...])
    return x + mod_ref[2:3] * m.reshape(x.shape)


def _mixer_prompt_kernel(x_ref, mod_ref, ng_ref, win_ref, cw_ref, woa_ref, lng_ref, wpair_ref,
                         sgb_ref, wob_ref, poolw_ref, pscale_ref, wo_ref,
                         o_ref, ctail_ref, ptail_ref, zc_ref, pc_ref, pd_ref, *, seq_step):
    tm = x_ref.shape[0] // SUB_TILES
    zh = zc_ref.shape[1] - tm
    ph = pc_ref.shape[1] - tm
    lax.fori_loop(0, SUB_TILES, functools.partial(
        _mixer_prompt_tile, x_ref, mod_ref, ng_ref, win_ref, cw_ref, woa_ref, lng_ref, wpair_ref,
        sgb_ref, wob_ref, poolw_ref, pscale_ref, wo_ref, o_ref, ctail_ref, ptail_ref, zc_ref,
        pc_ref, pd_ref, tm, zh, ph, seq_step), 0)


def _mixer_prompt_tile(x_ref, mod_ref, ng_ref, win_ref, cw_ref, woa_ref, lng_ref, wpair_ref,
                       sgb_ref, wob_ref, poolw_ref, pscale_ref, wo_ref, o_ref, ctail_ref,
                       ptail_ref, zc_ref, pc_ref, pd_ref, tm, zh, ph, seq_step, i, carry):
    t = seq_step * SUB_TILES + i

    @pl.when(t == 0)
    def _():
        zc_ref[:, 0:zh, :] = jnp.zeros((zc_ref.shape[0], zh, LANES), F32)
        pc_ref[:, 0:ph, :] = jnp.zeros((pc_ref.shape[0], ph, LANES), F32)

    def conv_fn(z):
        outs = []
        for g in range(zc_ref.shape[0]):
            ln = slice(g * LANES, (g + 1) * LANES)
            zg = z[:, ln]
            zc_ref[g, zh:, :] = zg
            acc = zg * cw_ref[CONV_W - 1:CONV_W, ln]
            for k in range(1, CONV_W):
                acc = acc + zc_ref[g, zh - k:zh - k + tm, :] * cw_ref[CONV_W - 1 - k:CONV_W - k, ln]
            tail = zg[tm - SUBLANES:]
            zc_ref[g, zh - SUBLANES:zh, :] = tail
            ctail_ref[:, ln] = tail
            outs.append(acc)
        return jnp.concatenate(outs, axis=1)

    def gate_fn(v):
        lane = lax.broadcasted_iota(jnp.int32, (CHUNK, LANES), 1)
        low = lane < (LANES // 2)
        n_blk = v.shape[1] // LANES
        rows = []
        for c in range(tm // CHUNK):
            vc = v[c * CHUNK:(c + 1) * CHUNK]
            outs = []
            for j in range(n_blk):
                blk = vc[:, j * LANES:(j + 1) * LANES]
                rhs = jnp.concatenate(
                    [jnp.where(low, blk, 0.0), jnp.where(low, 0.0, blk)], axis=0).astype(BF16)
                outs.append(_dot(wpair_ref[j], rhs))
            rows.append(jnp.concatenate(outs, axis=1) + sgb_ref[...])
        return jnp.concatenate(rows, axis=0)

    def pool_fn(p):
        pos = t * tm + lax.broadcasted_iota(jnp.int32, (MAX_WIN, 1), 0)
        outs = []
        buf = 0
        for gi, w in enumerate(POOL_WINDOWS):
            ln = slice(gi * LANES, (gi + 1) * LANES)
            pg = p[:, ln]
            pc_ref[gi, ph:, :] = pg
            src = pc_ref.at[gi]
            k = 1
            lo = 0
            while 2 * k < w:
                lo += SUBLANES
                dst = pd_ref.at[buf]
                buf += 1
                dst[lo:, :] = src[lo:, :] + src[lo - k:ph + tm - k, :]
                src = dst
                k *= 2
            s = src[ph:, :] + src[ph - k:ph + tm - k, :]
            head = s[:MAX_WIN] * (1.0 / jnp.minimum(pos + 1, w).astype(F32))
            outs.append(jnp.concatenate([head, s[MAX_WIN:] * (1.0 / w)], axis=0))
            tail = pg[tm - MAX_WIN:]
            pc_ref[gi, ph - MAX_WIN:ph, :] = tail
            ptail_ref[:, ln] = tail
        return jnp.concatenate(outs, axis=1)

    _rows(o_ref, i, tm)[...] = _mixer_core(
        _rows(x_ref, i, tm)[...], mod_ref, ng_ref, win_ref, woa_ref, lng_ref, wob_ref,
        poolw_ref, pscale_ref, wo_ref, conv_fn, gate_fn, pool_fn)
    return carry


def _mixer_sample_kernel(x_ref, mod_ref, ng_ref, win_ref, cw_ref, woa_ref, lng_ref, sgc_ref,
                         sgb_ref, wob_ref, poolw_ref, pscale_ref, wo_ref, hc_ref, hp_ref,
                         o_ref, nconv_ref, npool_ref, vopen_ref):
    bt = x_ref.shape[1] // SUB_TILES
    lax.fori_loop(0, SUB_TILES, functools.partial(
        _mixer_sample_tile, x_ref, mod_ref, ng_ref, win_ref, cw_ref, woa_ref, lng_ref, sgc_ref,
        sgb_ref, wob_ref, poolw_ref, pscale_ref, wo_ref, hc_ref, hp_ref, o_ref, nconv_ref,
        npool_ref, vopen_ref, bt), 0)


def _mixer_sample_tile(x_ref, mod_ref, ng_ref, win_ref, cw_ref, woa_ref, lng_ref, sgc_ref,
                       sgb_ref, wob_ref, poolw_ref, pscale_ref, wo_ref, hc_ref, hp_ref, o_ref,
                       nconv_ref, npool_ref, vopen_ref, bt, i, carry):
    x_ref, mod_ref, hc_ref, hp_ref, o_ref, nconv_ref, npool_ref, vopen_ref = (
        _rows(r, i, bt) for r in (x_ref, mod_ref, hc_ref, hp_ref, o_ref, nconv_ref, npool_ref,
                                  vopen_ref))
    steps = x_ref.shape[0]

    def conv_fn(z):
        z3 = z.reshape(steps, bt, z.shape[-1])
        ext = jnp.concatenate([hc_ref[...], z3], axis=0)
        nconv_ref[...] = ext[steps:]
        conv = sum(ext[k:k + steps] * cw_ref[k:k + 1] for k in range(CONV_W))
        return _flat(conv)

    def gate_fn(v):
        v3 = v.reshape(steps, bt, v.shape[-1])
        vopen_ref[...] = v3
        outs = []
        for tt in range(steps):
            acc = sgb_ref[tt:tt + 1] + sgc_ref[tt * steps:tt * steps + 1] * v3[0]
            for s in range(1, tt + 1):
                acc = acc + sgc_ref[tt * steps + s:tt * steps + s + 1] * v3[s]
            outs.append(acc)
        return _flat(jnp.stack(outs, axis=0))

    def pool_fn(p):
        p3 = p.reshape(steps, bt, p.shape[-1])
        ext = jnp.concatenate([hp_ref[...], p3], axis=0)
        hist = hp_ref.shape[0]
        npool_ref[...] = ext[steps:]
        outs = []
        for gi, w in enumerate(POOL_WINDOWS):
            s = ext[:, :, gi * LANES:(gi + 1) * LANES]
            k = 1
            while k < w:
                s = s[k:] + s[:-k]
                k *= 2
            first = hist - (w - 1)
            outs.append(jnp.stack(
                [s[first + tt] * (1.0 / min(PAST_LEN + tt + 1, w)) for tt in range(steps)], axis=0))
        return _flat(jnp.concatenate(outs, axis=-1))

    o_ref[...] = _mixer_core(x_ref[...], mod_ref, ng_ref, win_ref, woa_ref, lng_ref, wob_ref,
                             poolw_ref, pscale_ref, wo_ref, conv_fn, gate_fn, pool_fn)
    return carry


def _mixer_pair_kernel(xp_ref, xs_ref, modp_ref, mods_ref, ng_ref, win_ref, cw_ref, woa_ref, lng_ref,
                       wpair_ref, sgc_ref, sgb_ref, wob_ref, poolw_ref, pscale_ref, wo_ref,
                       hc_ref, hp_ref, op_ref, ctail_ref, ptail_ref, os_ref, nconv_ref, npool_ref,
                       vopen_ref, zc_ref, pc_ref, pd_ref, *, prompt_steps, per_seq):
    i = pl.program_id(0)

    @pl.when(i < prompt_steps)
    def _():
        _mixer_prompt_kernel(xp_ref, modp_ref, ng_ref, win_ref, cw_ref, woa_ref, lng_ref, wpair_ref,
                             sgb_ref, wob_ref, poolw_ref, pscale_ref, wo_ref, op_ref, ctail_ref,
                             ptail_ref, zc_ref, pc_ref, pd_ref, seq_step=i % per_seq)

    @pl.when(i >= prompt_steps)
    def _():
        _mixer_sample_kernel(xs_ref, mods_ref, ng_ref, win_ref, cw_ref, woa_ref, lng_ref, sgc_ref,
                             sgb_ref, wob_ref, poolw_ref, pscale_ref, wo_ref, hc_ref, hp_ref,
                             os_ref, nconv_ref, npool_ref, vopen_ref)


def _mixer_call(xp, xs, mod_p, mod_all, ng, win, cw, woa, lng, wpair, sgc, sgb, wob, poolw, pscale,
                wo, hconv, hpool, *, l):
    batch, seq, d = xp.shape
    steps, dec_batch, _ = xs.shape
    d_conv = cw.shape[-1]
    d_gmlp = wob.shape[1]
    d_pool = poolw.shape[2] * poolw.shape[1]
    sub = 1
    tm = TOKEN_TILE
    per_seq = seq // STEP_TOKENS
    prompt_steps = batch * per_seq
    bt = STEP_TOKENS // steps
    n_hc, n_hp = hconv.shape[1], hpool.shape[1]
    n_pool_tmp = sum(max(w.bit_length() - 2, 0) for w in POOL_WINDOWS)
    pi = lambda i: jnp.minimum(i, prompt_steps - 1)
    si = lambda i: jnp.maximum(i - prompt_steps, 0)
    full = lambda a: _const_spec((None,) + a.shape[1:], lambda i: (l,) + (0,) * (a.ndim - 1))
    xp_spec = pl.BlockSpec((None, STEP_TOKENS, d), lambda i: (pi(i) // per_seq, pi(i) % per_seq, 0))
    xs_spec = pl.BlockSpec((steps, bt, d), lambda i: (0, si(i), 0))
    return pl.pallas_call(
        functools.partial(_mixer_pair_kernel, prompt_steps=prompt_steps, per_seq=per_seq),
        grid=(prompt_steps + dec_batch // bt,),
        in_specs=[
            xp_spec,
            xs_spec,
            pl.BlockSpec((None, None, None, N_SUB, d), lambda i: (l, pi(i) // per_seq, sub, 0, 0)),
            pl.BlockSpec((None, N_SUB, bt, d), lambda i: (l, sub, si(i), 0)),
            pl.BlockSpec((None, None, 1, d), lambda i: (l, sub, 0, 0)),
            full(win), full(cw), full(woa), full(lng), full(wpair), full(sgc), full(sgb), full(wob),
            full(poolw), full(pscale), full(wo),
            pl.BlockSpec((None, n_hc, bt, d_conv), lambda i: (l, 0, si(i), 0)),
            pl.BlockSpec((None, n_hp, bt, d_pool), lambda i: (l, 0, si(i), 0)),
        ],
        out_specs=[
            xp_spec,
            pl.BlockSpec((None, SUBLANES, d_conv), lambda i: (pi(i) // per_seq, 0, 0)),
            pl.BlockSpec((None, MAX_WIN, d_pool), lambda i: (pi(i) // per_seq, 0, 0)),
            xs_spec,
            pl.BlockSpec((n_hc, bt, d_conv), lambda i: (0, si(i), 0)),
            pl.BlockSpec((n_hp, bt, d_pool), lambda i: (0, si(i), 0)),
            pl.BlockSpec((steps, bt, d_gmlp), lambda i: (0, si(i), 0)),
        ],
        out_shape=[
            jax.ShapeDtypeStruct(xp.shape, F32),
            jax.ShapeDtypeStruct((batch, SUBLANES, d_conv), F32),
            jax.ShapeDtypeStruct((batch, MAX_WIN, d_pool), F32),
            jax.ShapeDtypeStruct(xs.shape, F32),
            jax.ShapeDtypeStruct((n_hc, dec_batch, d_conv), F32),
            jax.ShapeDtypeStruct((n_hp, dec_batch, d_pool), F32),
            jax.ShapeDtypeStruct((steps, dec_batch, d_gmlp), F32),
        ],
        scratch_shapes=[
            pltpu.VMEM((d_conv // LANES, SUBLANES + tm, LANES), F32),
            pltpu.VMEM((d_pool // LANES, 2 * MAX_WIN + tm, LANES), F32),
            pltpu.VMEM((n_pool_tmp, 2 * MAX_WIN + tm, LANES), F32),
        ],
        compiler_params=pltpu.CompilerParams(
            dimension_semantics=("arbitrary",), vmem_limit_bytes=VMEM_LIMIT),
        name="mixer",
    )(xp, xs, mod_p, mod_all, ng, win, cw, woa, lng, wpair, sgc, sgb, wob, poolw, pscale, wo,
      hconv, hpool)


def kernel(x_prompt, x_sample, state_conv, state_pool, c_prompt, c_sample, norm_g, w_ada, b_ada,
           w1_gu, w1_dn, w2_gu, w2_dn, w_in, conv_w, w_out_a, ln_g, w_s, b_s, w_out_b,
           pool_w, pool_scale, w_o, final_norm_g):
    depth = w_in.shape[0]
    batch, seq, d = x_prompt.shape
    dec_batch, steps, _ = x_sample.shape
    d_gmlp = ln_g.shape[-1]
    head_dim = d_gmlp // G_HEADS
    assert seq % STEP_TOKENS == 0 and TOKEN_TILE % CHUNK == 0 and TOKEN_TILE % (steps * SUBLANES) == 0
    assert dec_batch % (STEP_TOKENS // steps) == 0 and steps <= CHUNK and 2 * head_dim == LANES

    c_all = jnp.concatenate([c_sample, c_prompt], axis=0)
    mod_all = _ada_call(c_all, w_ada, b_ada)
    mod_p = mod_all[:, :, dec_batch:, :].transpose(0, 2, 1, 3).reshape(depth, batch, N_SUB, 3, d)

    w1gu, w1dn, w2gu, w2dn = w1_gu, w1_dn, w2_gu, w2_dn
    assert w1_gu.shape[1] % GU_STAGE_ROWS == 0 and w1_dn.shape[1] % DN_STAGE_ROWS == 0
    win = w_in.astype(BF16)
    woa = w_out_a.astype(BF16)
    wob = w_out_b.astype(BF16)
    wo = w_o.astype(BF16)
    poolw = pool_w.astype(BF16)
    ng = norm_g.reshape(depth, N_SUB, 1, d)
    lng = ln_g.reshape(depth, 1, d_gmlp)
    pscale = pool_scale.reshape(depth, 1, d)
    fg = final_norm_g.reshape(1, d)
    w_tril = w_s * jnp.tril(jnp.ones((CHUNK, CHUNK), w_s.dtype))
    wpair = w_tril.reshape(depth, G_HEADS // 2, 2, CHUNK, CHUNK).transpose(0, 1, 3, 2, 4)
    wpair = wpair.reshape(depth, G_HEADS // 2, CHUNK, 2 * CHUNK).astype(BF16)
    sgb = jnp.repeat(b_s.transpose(0, 2, 1), head_dim, axis=-1)
    sgc = jnp.repeat(w_tril[:, :, :steps, :steps].transpose(0, 2, 3, 1), head_dim, axis=-1)
    sgc = sgc.reshape(depth, steps * steps, d_gmlp)

    xp = x_prompt
    xs = x_sample.transpose(1, 0, 2)
    hconv = state_conv.transpose(0, 2, 1, 3)
    hpool = state_pool.transpose(0, 2, 1, 3)

    conv_p, conv_s, pool_p, pool_s, v_s = [], [], [], [], []
    for l in range(depth):
        last = l == depth - 1
        xp, xs = _ffn_call(xp, xs, mod_p, mod_all, ng, w1gu, w1dn, fg, l=l, sub=0, final_norm=False)
        xp, ctail, ptail, xs, ncs, nps, vs = _mixer_call(
            xp, xs, mod_p, mod_all, ng, win, conv_w, woa, lng, wpair, sgc, sgb, wob, poolw, pscale,
            wo, hconv, hpool, l=l)
        xp, xs = _ffn_call(xp, xs, mod_p, mod_all, ng, w2gu, w2dn, fg, l=l, sub=2, final_norm=last)
        conv_p.append(ctail[:, SUBLANES - (CONV_W - 1):])
        pool_p.append(ptail[:, 1:])
        conv_s.append(ncs.transpose(1, 0, 2))
        pool_s.append(nps.transpose(1, 0, 2))
        v_s.append(vs.transpose(1, 0, 2))

    return (xp, xs.transpose(1, 0, 2), jnp.stack(conv_p), jnp.stack(conv_s), jnp.stack(pool_p),
            jnp.stack(pool_s), jnp.stack(v_s))
```

```python
import functools
import math

import jax
import jax.numpy as jnp
from jax import lax
from jax.experimental import pallas as pl
from jax.experimental.pallas import tpu as pltpu

F32 = jnp.float32
BF16 = jnp.bfloat16

EPS = 1e-6
N_SUB = 3
CONV_W = 3
G_HEADS = 8
CHUNK = 128
POOL_WINDOWS = (2, 4, 8, 16)
MAX_WIN = max(POOL_WINDOWS)
PAST_LEN = 16384

LANES = 128
SUBLANES = 8
MXU_DIM = 256
TOKEN_TILE = 512
SUB_TILES = 1
STEP_TOKENS = TOKEN_TILE * SUB_TILES
FF_CHUNKS = 2
GU_STAGE_ROWS = 128
DN_STAGE_ROWS = 256
STAGE_SLOTS = 3
IN_STAGE_ROWS = 32
MIXER_VMEM_LIMIT = 60 * 1024 * 1024
VMEM_LIMIT = 56 * 1024 * 1024


def _flat(a):
    return a.reshape(-1, a.shape[-1])


def _rms_mod(x, g, shift, scale):
    ms = jnp.mean(x * x, axis=-1, keepdims=True)
    y = x * lax.rsqrt(ms + EPS) * g
    return y * (1.0 + scale) + shift


_dot = functools.partial(jnp.dot, preferred_element_type=F32)


def _gelu(x):
    return 0.5 * x * (1.0 + lax.erf(x * (1.0 / math.sqrt(2.0))))


def _ada_kernel(c_ref, w_ref, b_ref, o_ref):
    c = c_ref[...]
    sc = (c * jax.nn.sigmoid(c)).astype(BF16)
    o_ref[...] = _dot(sc, w_ref[...].astype(BF16)) + b_ref[...]


def _ada_call(c_all, w_ada, b_ada):
    depth, d, cols = w_ada.shape
    n_mod = cols // d
    rows = c_all.shape[0]
    return pl.pallas_call(
        _ada_kernel,
        grid=(depth, n_mod),
        in_specs=[
            pl.BlockSpec((rows, d), lambda l, j: (0, 0)),
            pl.BlockSpec((None, d, d), lambda l, j: (l, 0, j)),
            pl.BlockSpec((None, None, 1, d), lambda l, j: (l, j, 0, 0)),
        ],
        out_specs=pl.BlockSpec((None, None, rows, d), lambda l, j: (l, j, 0, 0)),
        out_shape=jax.ShapeDtypeStruct((depth, n_mod, rows, d), F32),
        compiler_params=pltpu.CompilerParams(
            dimension_semantics=("arbitrary", "arbitrary"), vmem_limit_bytes=VMEM_LIMIT),
        name="ada_mod",
    )(c_all, w_ada, b_ada.reshape(depth, n_mod, 1, d))


def _ff_bounds(d_ff):
    tiles = d_ff // MXU_DIM
    per = -(-tiles // FF_CHUNKS)
    return [min(i * per, tiles) * MXU_DIM for i in range(FF_CHUNKS + 1)]


def _rows(ref, i, n):
    start = pl.multiple_of(i * n, n)
    return ref.at[(slice(None),) * (len(ref.shape) - 2) + (pl.ds(start, n),)]


def _ffn_kernel(x_ref, mod_ref, ng_ref, wgu_ref, wdn_ref, fg_ref, o_ref, *, final_norm, per_row_mod):
    n_sub = SUB_TILES
    rows = x_ref.shape[-2] // n_sub
    d_ff = wdn_ref.shape[0]
    bounds = _ff_bounds(d_ff)

    def tile(i, carry):
        x = _rows(x_ref, i, rows)[...]
        mod = _rows(mod_ref, i, rows) if per_row_mod else mod_ref
        h = _rms_mod(x, ng_ref[...], mod[0:1], mod[1:2])
        hb = _flat(h).astype(BF16)
        acc = None
        for c0, c1 in zip(bounds[:-1], bounds[1:]):
            a = _dot(hb, wgu_ref[:, c0:c1])
            b = _dot(hb, wgu_ref[:, d_ff + c0:d_ff + c1])
            g = (a * jax.nn.sigmoid(a)) * b
            part = _dot(g.astype(BF16), wdn_ref[c0:c1, :])
            acc = part if acc is None else acc + part
        y = x + (0.5 * mod[2:3]) * acc.reshape(x.shape)
        if final_norm:
            ms = jnp.mean(y * y, axis=-1, keepdims=True)
            y = y * lax.rsqrt(ms + EPS) * fg_ref[...]
        _rows(o_ref, i, rows)[...] = y
        return carry

    lax.fori_loop(0, n_sub, tile, 0)


def _const_spec(shape, index_map):
    return pl.BlockSpec(shape, index_map, pipeline_mode=pl.Buffered(1))


def _load_bf16(src, dst, stage, sems):
    slots, rpc = stage.shape[0], stage.shape[1]
    n = src.shape[0] // rpc

    def copy(c):
        slot = c % slots
        return pltpu.make_async_copy(src.at[pl.ds(c * rpc, rpc), :], stage.at[slot], sems.at[slot])

    for c in range(min(slots - 1, n)):
        copy(c).start()
    for c in range(n):
        if c + slots - 1 < n:
            copy(c + slots - 1).start()
        copy(c).wait()
        dst[c * rpc:(c + 1) * rpc, :] = stage[c % slots].astype(BF16)


def _ffn_pair_kernel(xp_ref, xs_ref, modp_ref, mods_ref, ng_ref, wgu_hbm, wdn_hbm, fg_ref,
                     op_ref, os_ref, wgu_ref, wdn_ref, gu_stage, dn_stage, sems,
                     *, final_norm, prompt_steps, layer):
    i = pl.program_id(0)

    @pl.when(i == 0)
    def _():
        _load_bf16(wgu_hbm.at[layer], wgu_ref, gu_stage, sems.at[0])
        _load_bf16(wdn_hbm.at[layer], wdn_ref, dn_stage, sems.at[1])

    @pl.when(i < prompt_steps)
    def _():
        _ffn_kernel(xp_ref, modp_ref, ng_ref, wgu_ref, wdn_ref, fg_ref, op_ref,
                    final_norm=final_norm, per_row_mod=False)

    @pl.when(i >= prompt_steps)
    def _():
        _ffn_kernel(xs_ref, mods_ref, ng_ref, wgu_ref, wdn_ref, fg_ref, os_ref,
                    final_norm=final_norm, per_row_mod=True)


def _ffn_call(xp, xs, mod_p, mod_all, ng, wgu, wdn, fg, *, l, sub, final_norm):
    batch, seq, d = xp.shape
    steps, dec_batch, _ = xs.shape
    two_ff = wgu.shape[-1]
    per_seq = seq // STEP_TOKENS
    prompt_steps = batch * per_seq
    bt = STEP_TOKENS // steps
    pi = lambda i: jnp.minimum(i, prompt_steps - 1)
    si = lambda i: jnp.maximum(i - prompt_steps, 0)
    xp_spec = pl.BlockSpec((None, STEP_TOKENS, d), lambda i: (pi(i) // per_seq, pi(i) % per_seq, 0))
    xs_spec = pl.BlockSpec((steps, bt, d), lambda i: (0, si(i), 0))
    return pl.pallas_call(
        functools.partial(_ffn_pair_kernel, final_norm=final_norm, prompt_steps=prompt_steps,
                          layer=l),
        grid=(prompt_steps + dec_batch // bt,),
        in_specs=[
            xp_spec,
            xs_spec,
            pl.BlockSpec((None, None, None, N_SUB, d), lambda i: (l, pi(i) // per_seq, sub, 0, 0)),
            pl.BlockSpec((None, N_SUB, bt, d), lambda i: (l, sub, si(i), 0)),
            pl.BlockSpec((None, None, 1, d), lambda i: (l, sub, 0, 0)),
            pl.BlockSpec(memory_space=pl.ANY),
            pl.BlockSpec(memory_space=pl.ANY),
            pl.BlockSpec((1, d), lambda i: (0, 0)),
        ],
        out_specs=[xp_spec, xs_spec],
        out_shape=[jax.ShapeDtypeStruct(xp.shape, F32), jax.ShapeDtypeStruct(xs.shape, F32)],
        scratch_shapes=[
            pltpu.VMEM((d, two_ff), BF16),
            pltpu.VMEM((two_ff // 2, d), BF16),
            pltpu.VMEM((STAGE_SLOTS, GU_STAGE_ROWS, two_ff), F32),
            pltpu.VMEM((STAGE_SLOTS, DN_STAGE_ROWS, d), F32),
            pltpu.SemaphoreType.DMA((2, STAGE_SLOTS)),
        ],
        compiler_params=pltpu.CompilerParams(
            dimension_semantics=("arbitrary",), vmem_limit_bytes=VMEM_LIMIT),
        name="ffn",
    )(xp, xs, mod_p, mod_all, ng, wgu, wdn, fg)


def _mixer_core(x, mod_ref, ng_ref, win_ref, woa_ref, lng_ref, wob_ref, poolw_ref, pscale_ref,
                wo_ref, conv_fn, gate_fn, pool_fn):
    d_conv = woa_ref.shape[0]
    d_gmlp = wob_ref.shape[0]
    d_pool = poolw_ref.shape[0] * poolw_ref.shape[1]
    d_model = wo_ref.shape[0]
    h = _rms_mod(x, ng_ref[...], mod_ref[0:1], mod_ref[1:2])
    hb = _flat(h).astype(BF16)

    win = lambda lo, n: win_ref[:, lo:lo + n]
    o_b = 3 * d_conv
    o_c = o_b + 2 * d_gmlp
    o_g = o_c + d_pool

    pa = _dot(hb, win(0, 3 * d_conv))
    xa = pa[:, :d_conv]
    bg = pa[:, d_conv:2 * d_conv]
    cg = pa[:, 2 * d_conv:]
    conv = conv_fn(cg * xa)
    a_in = (bg * conv).astype(BF16)

    puv = _dot(hb, win(o_b, 2 * d_gmlp))
    y_a = _dot(a_in, woa_ref[...])
    u = _gelu(puv[:, :d_gmlp])
    v = _gelu(puv[:, d_gmlp:])
    mu = jnp.mean(v, axis=-1, keepdims=True)
    vc = v - mu
    var = jnp.mean(vc * vc, axis=-1, keepdims=True)
    v = vc * lax.rsqrt(var + EPS) * lng_ref[...]

    p = _dot(hb, win(o_c, d_pool))
    merged = jax.nn.sigmoid(_dot(hb, win(o_g, d_model))) * y_a
    sg = gate_fn(v)
    gate_b = _dot(hb, win(o_g + d_model, d_model))

    diff = (pool_fn(p) - p).astype(BF16)
    gdim = poolw_ref.shape[1]
    y_b = _dot((u * sg).astype(BF16), wob_ref[...])
    y_c = jnp.concatenate(
        [_dot(diff[:, g * gdim:(g + 1) * gdim], poolw_ref[g]) for g in range(poolw_ref.shape[0])],
        axis=1) * pscale_ref[...]
    gate_c = _dot(hb, win(o_g + 2 * d_model, d_model))
    merged = merged + jax.nn.sigmoid(gate_b) * y_b + jax.nn.sigmoid(gate_c) * y_c
    m = _dot(merged.astype(BF16), wo_ref[...])
    return x + mod_ref[2:3] * m.reshape(x.shape)


def _mixer_prompt_kernel(x_ref, mod_ref, ng_ref, win_ref, cw_ref, woa_ref, lng_ref, wpair_ref,
                         sgb_ref, wob_ref, poolw_ref, pscale_ref, wo_ref,
                         o_ref, ctail_ref, ptail_ref, zc_ref, pc_ref, pd_ref, *, seq_step):
    tm = x_ref.shape[0] // SUB_TILES
    zh = zc_ref.shape[1] - tm
    ph = pc_ref.shape[1] - tm
    lax.fori_loop(0, SUB_TILES, functools.partial(
        _mixer_prompt_tile, x_ref, mod_ref, ng_ref, win_ref, cw_ref, woa_ref, lng_ref, wpair_ref,
        sgb_ref, wob_ref, poolw_ref, pscale_ref, wo_ref, o_ref, ctail_ref, ptail_ref, zc_ref,
        pc_ref, pd_ref, tm, zh, ph, seq_step), 0)


def _mixer_prompt_tile(x_ref, mod_ref, ng_ref, win_ref, cw_ref, woa_ref, lng_ref, wpair_ref,
                       sgb_ref, wob_ref, poolw_ref, pscale_ref, wo_ref, o_ref, ctail_ref,
                       ptail_ref, zc_ref, pc_ref, pd_ref, tm, zh, ph, seq_step, i, carry):
    t = seq_step * SUB_TILES + i

    @pl.when(t == 0)
    def _():
        zc_ref[:, 0:zh, :] = jnp.zeros((zc_ref.shape[0], zh, LANES), F32)
        pc_ref[:, 0:ph, :] = jnp.zeros((pc_ref.shape[0], ph, LANES), F32)

    def conv_fn(z):
        outs = []
        for g in range(zc_ref.shape[0]):
            ln = slice(g * LANES, (g + 1) * LANES)
            zg = z[:, ln]
            zc_ref[g, zh:, :] = zg
            acc = zg * cw_ref[CONV_W - 1:CONV_W, ln]
            for k in range(1, CONV_W):
                acc = acc + zc_ref[g, zh - k:zh - k + tm, :] * cw_ref[CONV_W - 1 - k:CONV_W - k, ln]
            tail = zg[tm - SUBLANES:]
            zc_ref[g, zh - SUBLANES:zh, :] = tail
            ctail_ref[:, ln] = tail
            outs.append(acc)
        return jnp.concatenate(outs, axis=1)

    def gate_fn(v):
        lane = lax.broadcasted_iota(jnp.int32, (CHUNK, LANES), 1)
        low = lane < (LANES // 2)
        n_blk = v.shape[1] // LANES
        rows = []
        for c in range(tm // CHUNK):
            vc = v[c * CHUNK:(c + 1) * CHUNK]
            outs = []
            for j in range(n_blk):
                blk = vc[:, j * LANES:(j + 1) * LANES]
                rhs = jnp.concatenate(
                    [jnp.where(low, blk, 0.0), jnp.where(low, 0.0, blk)], axis=0).astype(BF16)
                outs.append(_dot(wpair_ref[j], rhs))
            rows.append(jnp.concatenate(outs, axis=1) + sgb_ref[...])
        return jnp.concatenate(rows, axis=0)

    def pool_fn(p):
        pos = t * tm + lax.broadcasted_iota(jnp.int32, (MAX_WIN, 1), 0)
        outs = []
        buf = 0
        for gi, w in enumerate(POOL_WINDOWS):
            ln = slice(gi * LANES, (gi + 1) * LANES)
            pg = p[:, ln]
            pc_ref[gi, ph:, :] = pg
            src = pc_ref.at[gi]
            k = 1
            lo = 0
            while 2 * k < w:
                lo += SUBLANES
                dst = pd_ref.at[buf]
                buf += 1
                dst[lo:, :] = src[lo:, :] + src[lo - k:ph + tm - k, :]
                src = dst
                k *= 2
            s = src[ph:, :] + src[ph - k:ph + tm - k, :]
            head = s[:MAX_WIN] * (1.0 / jnp.minimum(pos + 1, w).astype(F32))
            outs.append(jnp.concatenate([head, s[MAX_WIN:] * (1.0 / w)], axis=0))
            tail = pg[tm - MAX_WIN:]
            pc_ref[gi, ph - MAX_WIN:ph, :] = tail
            ptail_ref[:, ln] = tail
        return jnp.concatenate(outs, axis=1)

    _rows(o_ref, i, tm)[...] = _mixer_core(
        _rows(x_ref, i, tm)[...], mod_ref, ng_ref, win_ref, woa_ref, lng_ref, wob_ref,
        poolw_ref, pscale_ref, wo_ref, conv_fn, gate_fn, pool_fn)
    return carry


def _mixer_sample_kernel(x_ref, mod_ref, ng_ref, win_ref, cw_ref, woa_ref, lng_ref, sgc_ref,
                         sgb_ref, wob_ref, poolw_ref, pscale_ref, wo_ref, hc_ref, hp_ref,
                         o_ref, nconv_ref, npool_ref, vopen_ref):
    bt = x_ref.shape[1] // SUB_TILES
    lax.fori_loop(0, SUB_TILES, functools.partial(
        _mixer_sample_tile, x_ref, mod_ref, ng_ref, win_ref, cw_ref, woa_ref, lng_ref, sgc_ref,
        sgb_ref, wob_ref, poolw_ref, pscale_ref, wo_ref, hc_ref, hp_ref, o_ref, nconv_ref,
        npool_ref, vopen_ref, bt), 0)


def _mixer_sample_tile(x_ref, mod_ref, ng_ref, win_ref, cw_ref, woa_ref, lng_ref, sgc_ref,
                       sgb_ref, wob_ref, poolw_ref, pscale_ref, wo_ref, hc_ref, hp_ref, o_ref,
                       nconv_ref, npool_ref, vopen_ref, bt, i, carry):
    x_ref, mod_ref, hc_ref, hp_ref, o_ref, nconv_ref, npool_ref, vopen_ref = (
        _rows(r, i, bt) for r in (x_ref, mod_ref, hc_ref, hp_ref, o_ref, nconv_ref, npool_ref,
                                  vopen_ref))
    steps = x_ref.shape[0]

    def conv_fn(z):
        z3 = z.reshape(steps, bt, z.shape[-1])
        ext = jnp.concatenate([hc_ref[...], z3], axis=0)
        nconv_ref[...] = ext[steps:]
        conv = sum(ext[k:k + steps] * cw_ref[k:k + 1] for k in range(CONV_W))
        return _flat(conv)

    def gate_fn(v):
        v3 = v.reshape(steps, bt, v.shape[-1])
        vopen_ref[...] = v3
        outs = []
        for tt in range(steps):
            acc = sgb_ref[tt:tt + 1] + sgc_ref[tt * steps:tt * steps + 1] * v3[0]
            for s in range(1, tt + 1):
                acc = acc + sgc_ref[tt * steps + s:tt * steps + s + 1] * v3[s]
            outs.append(acc)
        return _flat(jnp.stack(outs, axis=0))

    def pool_fn(p):
        p3 = p.reshape(steps, bt, p.shape[-1])
        ext = jnp.concatenate([hp_ref[...], p3], axis=0)
        hist = hp_ref.shape[0]
        npool_ref[...] = ext[steps:]
        outs = []
        for gi, w in enumerate(POOL_WINDOWS):
            s = ext[:, :, gi * LANES:(gi + 1) * LANES]
            k = 1
            while k < w:
                s = s[k:] + s[:-k]
                k *= 2
            first = hist - (w - 1)
            outs.append(jnp.stack(
                [s[first + tt] * (1.0 / min(PAST_LEN + tt + 1, w)) for tt in range(steps)], axis=0))
        return _flat(jnp.concatenate(outs, axis=-1))

    o_ref[...] = _mixer_core(x_ref[...], mod_ref, ng_ref, win_ref, woa_ref, lng_ref, wob_ref,
                             poolw_ref, pscale_ref, wo_ref, conv_fn, gate_fn, pool_fn)
    return carry


def _mixer_pair_kernel(xp_ref, xs_ref, modp_ref, mods_ref, ng_ref, win_hbm, cw_ref, woa_ref, lng_ref,
                       wpair_ref, sgc_ref, sgb_ref, wob_ref, poolw_ref, pscale_ref, wo_ref,
                       hc_ref, hp_ref, op_ref, ctail_ref, ptail_ref, os_ref, nconv_ref, npool_ref,
                       vopen_ref, zc_ref, pc_ref, pd_ref, win_ref, win_stage, win_sems,
                       *, prompt_steps, per_seq, layer):
    i = pl.program_id(0)

    @pl.when(i == 0)
    def _():
        _load_bf16(win_hbm.at[layer], win_ref, win_stage, win_sems)

    @pl.when(i < prompt_steps)
    def _():
        _mixer_prompt_kernel(xp_ref, modp_ref, ng_ref, win_ref, cw_ref, woa_ref, lng_ref, wpair_ref,
                             sgb_ref, wob_ref, poolw_ref, pscale_ref, wo_ref, op_ref, ctail_ref,
                             ptail_ref, zc_ref, pc_ref, pd_ref, seq_step=i % per_seq)

    @pl.when(i >= prompt_steps)
    def _():
        _mixer_sample_kernel(xs_ref, mods_ref, ng_ref, win_ref, cw_ref, woa_ref, lng_ref, sgc_ref,
                             sgb_ref, wob_ref, poolw_ref, pscale_ref, wo_ref, hc_ref, hp_ref,
                             os_ref, nconv_ref, npool_ref, vopen_ref)


def _mixer_call(xp, xs, mod_p, mod_all, ng, win, cw, woa, lng, wpair, sgc, sgb, wob, poolw, pscale,
                wo, hconv, hpool, *, l):
    batch, seq, d = xp.shape
    steps, dec_batch, _ = xs.shape
    d_conv = cw.shape[-1]
    d_gmlp = wob.shape[1]
    d_pool = poolw.shape[2] * poolw.shape[1]
    sub = 1
    tm = TOKEN_TILE
    per_seq = seq // STEP_TOKENS
    prompt_steps = batch * per_seq
    bt = STEP_TOKENS // steps
    n_hc, n_hp = hconv.shape[1], hpool.shape[1]
    n_pool_tmp = sum(max(w.bit_length() - 2, 0) for w in POOL_WINDOWS)
    pi = lambda i: jnp.minimum(i, prompt_steps - 1)
    si = lambda i: jnp.maximum(i - prompt_steps, 0)
    full = lambda a: _const_spec((None,) + a.shape[1:], lambda i: (l,) + (0,) * (a.ndim - 1))
    xp_spec = pl.BlockSpec((None, STEP_TOKENS, d), lambda i: (pi(i) // per_seq, pi(i) % per_seq, 0))
    xs_spec = pl.BlockSpec((steps, bt, d), lambda i: (0, si(i), 0))
    return pl.pallas_call(
        functools.partial(_mixer_pair_kernel, prompt_steps=prompt_steps, per_seq=per_seq, layer=l),
        grid=(prompt_steps + dec_batch // bt,),
        in_specs=[
            xp_spec,
            xs_spec,
            pl.BlockSpec((None, None, None, N_SUB, d), lambda i: (l, pi(i) // per_seq, sub, 0, 0)),
            pl.BlockSpec((None, N_SUB, bt, d), lambda i: (l, sub, si(i), 0)),
            pl.BlockSpec((None, None, 1, d), lambda i: (l, sub, 0, 0)),
            pl.BlockSpec(memory_space=pl.ANY),
            full(cw), full(woa), full(lng), full(wpair), full(sgc), full(sgb), full(wob),
            full(poolw), full(pscale), full(wo),
            pl.BlockSpec((None, n_hc, bt, d_conv), lambda i: (l, 0, si(i), 0)),
            pl.BlockSpec((None, n_hp, bt, d_pool), lambda i: (l, 0, si(i), 0)),
        ],
        out_specs=[
            xp_spec,
            pl.BlockSpec((None, SUBLANES, d_conv), lambda i: (pi(i) // per_seq, 0, 0)),
            pl.BlockSpec((None, MAX_WIN, d_pool), lambda i: (pi(i) // per_seq, 0, 0)),
            xs_spec,
            pl.BlockSpec((n_hc, bt, d_conv), lambda i: (0, si(i), 0)),
            pl.BlockSpec((n_hp, bt, d_pool), lambda i: (0, si(i), 0)),
            pl.BlockSpec((steps, bt, d_gmlp), lambda i: (0, si(i), 0)),
        ],
        out_shape=[
            jax.ShapeDtypeStruct(xp.shape, F32),
            jax.ShapeDtypeStruct((batch, SUBLANES, d_conv), F32),
            jax.ShapeDtypeStruct((batch, MAX_WIN, d_pool), F32),
            jax.ShapeDtypeStruct(xs.shape, F32),
            jax.ShapeDtypeStruct((n_hc, dec_batch, d_conv), F32),
            jax.ShapeDtypeStruct((n_hp, dec_batch, d_pool), F32),
            jax.ShapeDtypeStruct((steps, dec_batch, d_gmlp), F32),
        ],
        scratch_shapes=[
            pltpu.VMEM((d_conv // LANES, SUBLANES + tm, LANES), F32),
            pltpu.VMEM((d_pool // LANES, 2 * MAX_WIN + tm, LANES), F32),
            pltpu.VMEM((n_pool_tmp, 2 * MAX_WIN + tm, LANES), F32),
            pltpu.VMEM(win.shape[1:], BF16),
            pltpu.VMEM((STAGE_SLOTS, IN_STAGE_ROWS, win.shape[2]), F32),
            pltpu.SemaphoreType.DMA((STAGE_SLOTS,)),
        ],
        compiler_params=pltpu.CompilerParams(
            dimension_semantics=("arbitrary",), vmem_limit_bytes=MIXER_VMEM_LIMIT),
        name="mixer",
    )(xp, xs, mod_p, mod_all, ng, win, cw, woa, lng, wpair, sgc, sgb, wob, poolw, pscale, wo,
      hconv, hpool)


def kernel(x_prompt, x_sample, state_conv, state_pool, c_prompt, c_sample, norm_g, w_ada, b_ada,
           w1_gu, w1_dn, w2_gu, w2_dn, w_in, conv_w, w_out_a, ln_g, w_s, b_s, w_out_b,
           pool_w, pool_scale, w_o, final_norm_g):
    depth = w_in.shape[0]
    batch, seq, d = x_prompt.shape
    dec_batch, steps, _ = x_sample.shape
    d_gmlp = ln_g.shape[-1]
    head_dim = d_gmlp // G_HEADS
    assert seq % STEP_TOKENS == 0 and TOKEN_TILE % CHUNK == 0 and TOKEN_TILE % (steps * SUBLANES) == 0
    assert dec_batch % (STEP_TOKENS // steps) == 0 and steps <= CHUNK and 2 * head_dim == LANES

    c_all = jnp.concatenate([c_sample, c_prompt], axis=0)
    mod_all = _ada_call(c_all, w_ada, b_ada)
    mod_p = mod_all[:, :, dec_batch:, :].transpose(0, 2, 1, 3).reshape(depth, batch, N_SUB, 3, d)

    w1gu, w1dn, w2gu, w2dn = w1_gu, w1_dn, w2_gu, w2_dn
    assert w1_gu.shape[1] % GU_STAGE_ROWS == 0 and w1_dn.shape[1] % DN_STAGE_ROWS == 0
    win = w_in
    assert w_in.shape[1] % IN_STAGE_ROWS == 0
    woa = w_out_a.astype(BF16)
    wob = w_out_b.astype(BF16)
    wo = w_o.astype(BF16)
    poolw = pool_w.astype(BF16)
    ng = norm_g.reshape(depth, N_SUB, 1, d)
    lng = ln_g.reshape(depth, 1, d_gmlp)
    pscale = pool_scale.reshape(depth, 1, d)
    fg = final_norm_g.reshape(1, d)
    w_tril = w_s * jnp.tril(jnp.ones((CHUNK, CHUNK), w_s.dtype))
    wpair = w_tril.reshape(depth, G_HEADS // 2, 2, CHUNK, CHUNK).transpose(0, 1, 3, 2, 4)
    wpair = wpair.reshape(depth, G_HEADS // 2, CHUNK, 2 * CHUNK).astype(BF16)
    sgb = jnp.repeat(b_s.transpose(0, 2, 1), head_dim, axis=-1)
    sgc = jnp.repeat(w_tril[:, :, :steps, :steps].transpose(0, 2, 3, 1), head_dim, axis=-1)
    sgc = sgc.reshape(depth, steps * steps, d_gmlp)

    xp = x_prompt
    xs = x_sample.transpose(1, 0, 2)
    hconv = state_conv.transpose(0, 2, 1, 3)
    hpool = state_pool.transpose(0, 2, 1, 3)

    conv_p, conv_s, pool_p, pool_s, v_s = [], [], [], [], []
    for l in range(depth):
        last = l == depth - 1
        xp, xs = _ffn_call(xp, xs, mod_p, mod_all, ng, w1gu, w1dn, fg, l=l, sub=0, final_norm=False)
        xp, ctail, ptail, xs, ncs, nps, vs = _mixer_call(
            xp, xs, mod_p, mod_all, ng, win, conv_w, woa, lng, wpair, sgc, sgb, wob, poolw, pscale,
            wo, hconv, hpool, l=l)
        xp, xs = _ffn_call(xp, xs, mod_p, mod_all, ng, w2gu, w2dn, fg, l=l, sub=2, final_norm=last)
        conv_p.append(ctail[:, SUBLANES - (CONV_W - 1):])
        pool_p.append(ptail[:, 1:])
        conv_s.append(ncs.transpose(1, 0, 2))
        pool_s.append(nps.transpose(1, 0, 2))
        v_s.append(vs.transpose(1, 0, 2))

    return (xp, xs.transpose(1, 0, 2), jnp.stack(conv_p), jnp.stack(conv_s), jnp.stack(pool_p),
            jnp.stack(pool_s), jnp.stack(v_s))
```

```python
import functools
import math

import jax
import jax.numpy as jnp
from jax import lax
from jax.experimental import pallas as pl
from jax.experimental.pallas import tpu as pltpu

F32 = jnp.float32
BF16 = jnp.bfloat16

EPS = 1e-6
N_SUB = 3
CONV_W = 3
G_HEADS = 8
CHUNK = 128
POOL_WINDOWS = (2, 4, 8, 16)
MAX_WIN = max(POOL_WINDOWS)
PAST_LEN = 16384

LANES = 128
SUBLANES = 8
MXU_DIM = 256
TOKEN_TILE = 512
SUB_TILES = 1
STEP_TOKENS = TOKEN_TILE * SUB_TILES
FF_CHUNKS = 2
GU_STAGE_ROWS = 128
DN_STAGE_ROWS = 256
STAGE_SLOTS = 4
VMEM_LIMIT = 56 * 1024 * 1024


def _flat(a):
    return a.reshape(-1, a.shape[-1])


def _rms_mod(x, g, shift, scale):
    ms = jnp.mean(x * x, axis=-1, keepdims=True)
    y = x * lax.rsqrt(ms + EPS) * g
    return y * (1.0 + scale) + shift


_dot = functools.partial(jnp.dot, preferred_element_type=F32)


def _gelu(x):
    return 0.5 * x * (1.0 + lax.erf(x * (1.0 / math.sqrt(2.0))))


def _ada_kernel(c_ref, w_ref, b_ref, o_ref):
    c = c_ref[...]
    sc = (c * jax.nn.sigmoid(c)).astype(BF16)
    o_ref[...] = _dot(sc, w_ref[...].astype(BF16)) + b_ref[...]


def _ada_call(c_all, w_ada, b_ada):
    depth, d, cols = w_ada.shape
    n_mod = cols // d
    rows = c_all.shape[0]
    return pl.pallas_call(
        _ada_kernel,
        grid=(depth, n_mod),
        in_specs=[
            pl.BlockSpec((rows, d), lambda l, j: (0, 0)),
            pl.BlockSpec((None, d, d), lambda l, j: (l, 0, j)),
            pl.BlockSpec((None, None, 1, d), lambda l, j: (l, j, 0, 0)),
        ],
        out_specs=pl.BlockSpec((None, None, rows, d), lambda l, j: (l, j, 0, 0)),
        out_shape=jax.ShapeDtypeStruct((depth, n_mod, rows, d), F32),
        compiler_params=pltpu.CompilerParams(
            dimension_semantics=("arbitrary", "arbitrary"), vmem_limit_bytes=VMEM_LIMIT),
        name="ada_mod",
    )(c_all, w_ada, b_ada.reshape(depth, n_mod, 1, d))


def _ff_bounds(d_ff):
    tiles = d_ff // MXU_DIM
    per = -(-tiles // FF_CHUNKS)
    return [min(i * per, tiles) * MXU_DIM for i in range(FF_CHUNKS + 1)]


def _rows(ref, i, n):
    start = pl.multiple_of(i * n, n)
    return ref.at[(slice(None),) * (len(ref.shape) - 2) + (pl.ds(start, n),)]


def _ffn_kernel(x_ref, mod_ref, ng_ref, wgu_ref, wdn_ref, fg_ref, o_ref, *, final_norm, per_row_mod):
    n_sub = SUB_TILES
    rows = x_ref.shape[-2] // n_sub
    d_ff = wdn_ref.shape[0]
    bounds = _ff_bounds(d_ff)

    def tile(i, carry):
        x = _rows(x_ref, i, rows)[...]
        mod = _rows(mod_ref, i, rows) if per_row_mod else mod_ref
        h = _rms_mod(x, ng_ref[...], mod[0:1], mod[1:2])
        hb = _flat(h).astype(BF16)
        acc = None
        for c0, c1 in zip(bounds[:-1], bounds[1:]):
            a = _dot(hb, wgu_ref[:, c0:c1])
            b = _dot(hb, wgu_ref[:, d_ff + c0:d_ff + c1])
            g = (a * jax.nn.sigmoid(a)) * b
            part = _dot(g.astype(BF16), wdn_ref[c0:c1, :])
            acc = part if acc is None else acc + part
        y = x + (0.5 * mod[2:3]) * acc.reshape(x.shape)
        if final_norm:
            ms = jnp.mean(y * y, axis=-1, keepdims=True)
            y = y * lax.rsqrt(ms + EPS) * fg_ref[...]
        _rows(o_ref, i, rows)[...] = y
        return carry

    lax.fori_loop(0, n_sub, tile, 0)


def _const_spec(shape, index_map):
    return pl.BlockSpec(shape, index_map, pipeline_mode=pl.Buffered(1))


def _load_bf16(src, dst, stage, sems):
    slots, rpc = stage.shape[0], stage.shape[1]
    n = src.shape[0] // rpc

    def copy(c):
        slot = c % slots
        return pltpu.make_async_copy(src.at[pl.ds(c * rpc, rpc), :], stage.at[slot], sems.at[slot])

    for c in range(min(slots - 1, n)):
        copy(c).start()
    for c in range(n):
        if c + slots - 1 < n:
            copy(c + slots - 1).start()
        copy(c).wait()
        dst[c * rpc:(c + 1) * rpc, :] = stage[c % slots].astype(BF16)


def _ffn_pair_kernel(xp_ref, xs_ref, modp_ref, mods_ref, ng_ref, wgu_hbm, wdn_hbm, fg_ref,
                     op_ref, os_ref, wgu_ref, wdn_ref, gu_stage, dn_stage, sems,
                     *, final_norm, prompt_steps, layer):
    i = pl.program_id(0)

    @pl.when(i == 0)
    def _():
        _load_bf16(wgu_hbm.at[layer], wgu_ref, gu_stage, sems.at[0])
        _load_bf16(wdn_hbm.at[layer], wdn_ref, dn_stage, sems.at[1])

    @pl.when(i < prompt_steps)
    def _():
        _ffn_kernel(xp_ref, modp_ref, ng_ref, wgu_ref, wdn_ref, fg_ref, op_ref,
                    final_norm=final_norm, per_row_mod=False)

    @pl.when(i >= prompt_steps)
    def _():
        _ffn_kernel(xs_ref, mods_ref, ng_ref, wgu_ref, wdn_ref, fg_ref, os_ref,
                    final_norm=final_norm, per_row_mod=True)


def _ffn_call(xp, xs, mod_p, mod_all, ng, wgu, wdn, fg, *, l, sub, final_norm):
    batch, seq, d = xp.shape
    steps, dec_batch, _ = xs.shape
    two_ff = wgu.shape[-1]
    per_seq = seq // STEP_TOKENS
    prompt_steps = batch * per_seq
    bt = STEP_TOKENS // steps
    pi = lambda i: jnp.minimum(i, prompt_steps - 1)
    si = lambda i: jnp.maximum(i - prompt_steps, 0)
    xp_spec = pl.BlockSpec((None, STEP_TOKENS, d), lambda i: (pi(i) // per_seq, pi(i) % per_seq, 0))
    xs_spec = pl.BlockSpec((steps, bt, d), lambda i: (0, si(i), 0))
    return pl.pallas_call(
        functools.partial(_ffn_pair_kernel, final_norm=final_norm, prompt_steps=prompt_steps,
                          layer=l),
        grid=(prompt_steps + dec_batch // bt,),
        in_specs=[
            xp_spec,
            xs_spec,
            pl.BlockSpec((None, None, None, N_SUB, d), lambda i: (l, pi(i) // per_seq, sub, 0, 0)),
            pl.BlockSpec((None, N_SUB, bt, d), lambda i: (l, sub, si(i), 0)),
            pl.BlockSpec((None, None, 1, d), lambda i: (l, sub, 0, 0)),
            pl.BlockSpec(memory_space=pl.ANY),
            pl.BlockSpec(memory_space=pl.ANY),
            pl.BlockSpec((1, d), lambda i: (0, 0)),
        ],
        out_specs=[xp_spec, xs_spec],
        out_shape=[jax.ShapeDtypeStruct(xp.shape, F32), jax.ShapeDtypeStruct(xs.shape, F32)],
        scratch_shapes=[
            pltpu.VMEM((d, two_ff), BF16),
            pltpu.VMEM((two_ff // 2, d), BF16),
            pltpu.VMEM((STAGE_SLOTS, GU_STAGE_ROWS, two_ff), F32),
            pltpu.VMEM((STAGE_SLOTS, DN_STAGE_ROWS, d), F32),
            pltpu.SemaphoreType.DMA((2, STAGE_SLOTS)),
        ],
        compiler_params=pltpu.CompilerParams(
            dimension_semantics=("arbitrary",), vmem_limit_bytes=VMEM_LIMIT),
        name="ffn",
    )(xp, xs, mod_p, mod_all, ng, wgu, wdn, fg)


def _mixer_core(x, mod_ref, ng_ref, win_ref, woa_ref, lng_ref, wob_ref, poolw_ref, pscale_ref,
                wo_ref, conv_fn, gate_fn, pool_fn):
    d_conv = woa_ref.shape[0]
    d_gmlp = wob_ref.shape[0]
    d_pool = poolw_ref.shape[0] * poolw_ref.shape[1]
    d_model = wo_ref.shape[0]
    h = _rms_mod(x, ng_ref[...], mod_ref[0:1], mod_ref[1:2])
    hb = _flat(h).astype(BF16)

    win = lambda lo, n: win_ref[:, lo:lo + n]
    o_b = 3 * d_conv
    o_c = o_b + 2 * d_gmlp
    o_g = o_c + d_pool

    pa = _dot(hb, win(0, 3 * d_conv))
    xa = pa[:, :d_conv]
    bg = pa[:, d_conv:2 * d_conv]
    cg = pa[:, 2 * d_conv:]
    conv = conv_fn(cg * xa)
    a_in = (bg * conv).astype(BF16)

    puv = _dot(hb, win(o_b, 2 * d_gmlp))
    y_a = _dot(a_in, woa_ref[...])
    u = _gelu(puv[:, :d_gmlp])
    v = _gelu(puv[:, d_gmlp:])
    mu = jnp.mean(v, axis=-1, keepdims=True)
    vc = v - mu
    var = jnp.mean(vc * vc, axis=-1, keepdims=True)
    v = vc * lax.rsqrt(var + EPS) * lng_ref[...]

    p = _dot(hb, win(o_c, d_pool))
    merged = jax.nn.sigmoid(_dot(hb, win(o_g, d_model))) * y_a
    sg = gate_fn(v)
    gate_b = _dot(hb, win(o_g + d_model, d_model))

    diff = (pool_fn(p) - p).astype(BF16)
    gdim = poolw_ref.shape[1]
    y_b = _dot((u * sg).astype(BF16), wob_ref[...])
    y_c = jnp.concatenate(
        [_dot(diff[:, g * gdim:(g + 1) * gdim], poolw_ref[g]) for g in range(poolw_ref.shape[0])],
        axis=1) * pscale_ref[...]
    gate_c = _dot(hb, win(o_g + 2 * d_model, d_model))
    merged = merged + jax.nn.sigmoid(gate_b) * y_b + jax.nn.sigmoid(gate_c) * y_c
    m = _dot(merged.astype(BF16), wo_ref[...])
    return x + mod_ref[2:3] * m.reshape(x.shape)


def _mixer_prompt_kernel(x_ref, mod_ref, ng_ref, win_ref, cw_ref, woa_ref, lng_ref, wpair_ref,
                         sgb_ref, wob_ref, poolw_ref, pscale_ref, wo_ref,
                         o_ref, ctail_ref, ptail_ref, zc_ref, pc_ref, pd_ref, *, seq_step):
    tm = x_ref.shape[0] // SUB_TILES
    zh = zc_ref.shape[1] - tm
    ph = pc_ref.shape[1] - tm
    lax.fori_loop(0, SUB_TILES, functools.partial(
        _mixer_prompt_tile, x_ref, mod_ref, ng_ref, win_ref, cw_ref, woa_ref, lng_ref, wpair_ref,
        sgb_ref, wob_ref, poolw_ref, pscale_ref, wo_ref, o_ref, ctail_ref, ptail_ref, zc_ref,
        pc_ref, pd_ref, tm, zh, ph, seq_step), 0)


def _mixer_prompt_tile(x_ref, mod_ref, ng_ref, win_ref, cw_ref, woa_ref, lng_ref, wpair_ref,
                       sgb_ref, wob_ref, poolw_ref, pscale_ref, wo_ref, o_ref, ctail_ref,
                       ptail_ref, zc_ref, pc_ref, pd_ref, tm, zh, ph, seq_step, i, carry):
    t = seq_step * SUB_TILES + i

    @pl.when(t == 0)
    def _():
        zc_ref[:, 0:zh, :] = jnp.zeros((zc_ref.shape[0], zh, LANES), F32)
        pc_ref[:, 0:ph, :] = jnp.zeros((pc_ref.shape[0], ph, LANES), F32)

    def conv_fn(z):
        outs = []
        for g in range(zc_ref.shape[0]):
            ln = slice(g * LANES, (g + 1) * LANES)
            zg = z[:, ln]
            zc_ref[g, zh:, :] = zg
            acc = zg * cw_ref[CONV_W - 1:CONV_W, ln]
            for k in range(1, CONV_W):
                acc = acc + zc_ref[g, zh - k:zh - k + tm, :] * cw_ref[CONV_W - 1 - k:CONV_W - k, ln]
            tail = zg[tm - SUBLANES:]
            zc_ref[g, zh - SUBLANES:zh, :] = tail
            ctail_ref[:, ln] = tail
            outs.append(acc)
        return jnp.concatenate(outs, axis=1)

    def gate_fn(v):
        lane = lax.broadcasted_iota(jnp.int32, (CHUNK, LANES), 1)
        low = lane < (LANES // 2)
        n_blk = v.shape[1] // LANES
        rows = []
        for c in range(tm // CHUNK):
            vc = v[c * CHUNK:(c + 1) * CHUNK]
            outs = []
            for j in range(n_blk):
                blk = vc[:, j * LANES:(j + 1) * LANES]
                rhs = jnp.concatenate(
                    [jnp.where(low, blk, 0.0), jnp.where(low, 0.0, blk)], axis=0).astype(BF16)
                outs.append(_dot(wpair_ref[j], rhs))
            rows.append(jnp.concatenate(outs, axis=1) + sgb_ref[...])
        return jnp.concatenate(rows, axis=0)

    def pool_fn(p):
        pos = t * tm + lax.broadcasted_iota(jnp.int32, (MAX_WIN, 1), 0)
        outs = []
        buf = 0
        for gi, w in enumerate(POOL_WINDOWS):
            ln = slice(gi * LANES, (gi + 1) * LANES)
            pg = p[:, ln]
            pc_ref[gi, ph:, :] = pg
            src = pc_ref.at[gi]
            k = 1
            lo = 0
            while 2 * k < w:
                lo += SUBLANES
                dst = pd_ref.at[buf]
                buf += 1
                dst[lo:, :] = src[lo:, :] + src[lo - k:ph + tm - k, :]
                src = dst
                k *= 2
            s = src[ph:, :] + src[ph - k:ph + tm - k, :]
            head = s[:MAX_WIN] * (1.0 / jnp.minimum(pos + 1, w).astype(F32))
            outs.append(jnp.concatenate([head, s[MAX_WIN:] * (1.0 / w)], axis=0))
            tail = pg[tm - MAX_WIN:]
            pc_ref[gi, ph - MAX_WIN:ph, :] = tail
            ptail_ref[:, ln] = tail
        return jnp.concatenate(outs, axis=1)

    _rows(o_ref, i, tm)[...] = _mixer_core(
        _rows(x_ref, i, tm)[...], mod_ref, ng_ref, win_ref, woa_ref, lng_ref, wob_ref,
        poolw_ref, pscale_ref, wo_ref, conv_fn, gate_fn, pool_fn)
    return carry


def _mixer_sample_kernel(x_ref, mod_ref, ng_ref, win_ref, cw_ref, woa_ref, lng_ref, sgc_ref,
                         sgb_ref, wob_ref, poolw_ref, pscale_ref, wo_ref, hc_ref, hp_ref,
                         o_ref, nconv_ref, npool_ref, vopen_ref):
    bt = x_ref.shape[1] // SUB_TILES
    lax.fori_loop(0, SUB_TILES, functools.partial(
        _mixer_sample_tile, x_ref, mod_ref, ng_ref, win_ref, cw_ref, woa_ref, lng_ref, sgc_ref,
        sgb_ref, wob_ref, poolw_ref, pscale_ref, wo_ref, hc_ref, hp_ref, o_ref, nconv_ref,
        npool_ref, vopen_ref, bt), 0)


def _mixer_sample_tile(x_ref, mod_ref, ng_ref, win_ref, cw_ref, woa_ref, lng_ref, sgc_ref,
                       sgb_ref, wob_ref, poolw_ref, pscale_ref, wo_ref, hc_ref, hp_ref, o_ref,
                       nconv_ref, npool_ref, vopen_ref, bt, i, carry):
    x_ref, mod_ref, hc_ref, hp_ref, o_ref, nconv_ref, npool_ref, vopen_ref = (
        _rows(r, i, bt) for r in (x_ref, mod_ref, hc_ref, hp_ref, o_ref, nconv_ref, npool_ref,
                                  vopen_ref))
    steps = x_ref.shape[0]

    def conv_fn(z):
        z3 = z.reshape(steps, bt, z.shape[-1])
        ext = jnp.concatenate([hc_ref[...], z3], axis=0)
        nconv_ref[...] = ext[steps:]
        conv = sum(ext[k:k + steps] * cw_ref[k:k + 1] for k in range(CONV_W))
        return _flat(conv)

    def gate_fn(v):
        v3 = v.reshape(steps, bt, v.shape[-1])
        vopen_ref[...] = v3
        outs = []
        for tt in range(steps):
            acc = sgb_ref[tt:tt + 1] + sgc_ref[tt * steps:tt * steps + 1] * v3[0]
            for s in range(1, tt + 1):
                acc = acc + sgc_ref[tt * steps + s:tt * steps + s + 1] * v3[s]
            outs.append(acc)
        return _flat(jnp.stack(outs, axis=0))

    def pool_fn(p):
        p3 = p.reshape(steps, bt, p.shape[-1])
        ext = jnp.concatenate([hp_ref[...], p3], axis=0)
        hist = hp_ref.shape[0]
        npool_ref[...] = ext[steps:]
        outs = []
        for gi, w in enumerate(POOL_WINDOWS):
            s = ext[:, :, gi * LANES:(gi + 1) * LANES]
            k = 1
            while k < w:
                s = s[k:] + s[:-k]
                k *= 2
            first = hist - (w - 1)
            outs.append(jnp.stack(
                [s[first + tt] * (1.0 / min(PAST_LEN + tt + 1, w)) for tt in range(steps)], axis=0))
        return _flat(jnp.concatenate(outs, axis=-1))

    o_ref[...] = _mixer_core(x_ref[...], mod_ref, ng_ref, win_ref, woa_ref, lng_ref, wob_ref,
                             poolw_ref, pscale_ref, wo_ref, conv_fn, gate_fn, pool_fn)
    return carry


def _mixer_pair_kernel(xp_ref, xs_ref, modp_ref, mods_ref, ng_ref, win_ref, cw_ref, woa_ref, lng_ref,
                       wpair_ref, sgc_ref, sgb_ref, wob_ref, poolw_ref, pscale_ref, wo_ref,
                       hc_ref, hp_ref, op_ref, ctail_ref, ptail_ref, os_ref, nconv_ref, npool_ref,
                       vopen_ref, zc_ref, pc_ref, pd_ref, *, prompt_steps, per_seq):
    i = pl.program_id(0)

    @pl.when(i < prompt_steps)
    def _():
        _mixer_prompt_kernel(xp_ref, modp_ref, ng_ref, win_ref, cw_ref, woa_ref, lng_ref, wpair_ref,
                             sgb_ref, wob_ref, poolw_ref, pscale_ref, wo_ref, op_ref, ctail_ref,
                             ptail_ref, zc_ref, pc_ref, pd_ref, seq_step=i % per_seq)

    @pl.when(i >= prompt_steps)
    def _():
        _mixer_sample_kernel(xs_ref, mods_ref, ng_ref, win_ref, cw_ref, woa_ref, lng_ref, sgc_ref,
                             sgb_ref, wob_ref, poolw_ref, pscale_ref, wo_ref, hc_ref, hp_ref,
                             os_ref, nconv_ref, npool_ref, vopen_ref)


def _mixer_call(xp, xs, mod_p, mod_all, ng, win, cw, woa, lng, wpair, sgc, sgb, wob, poolw, pscale,
                wo, hconv, hpool, *, l):
    batch, seq, d = xp.shape
    steps, dec_batch, _ = xs.shape
    d_conv = cw.shape[-1]
    d_gmlp = wob.shape[1]
    d_pool = poolw.shape[2] * poolw.shape[1]
    sub = 1
    tm = TOKEN_TILE
    per_seq = seq // STEP_TOKENS
    prompt_steps = batch * per_seq
    bt = STEP_TOKENS // steps
    n_hc, n_hp = hconv.shape[1], hpool.shape[1]
    n_pool_tmp = sum(max(w.bit_length() - 2, 0) for w in POOL_WINDOWS)
    pi = lambda i: jnp.minimum(i, prompt_steps - 1)
    si = lambda i: jnp.maximum(i - prompt_steps, 0)
    full = lambda a: _const_spec((None,) + a.shape[1:], lambda i: (l,) + (0,) * (a.ndim - 1))
    xp_spec = pl.BlockSpec((None, STEP_TOKENS, d), lambda i: (pi(i) // per_seq, pi(i) % per_seq, 0))
    xs_spec = pl.BlockSpec((steps, bt, d), lambda i: (0, si(i), 0))
    return pl.pallas_call(
        functools.partial(_mixer_pair_kernel, prompt_steps=prompt_steps, per_seq=per_seq),
        grid=(prompt_steps + dec_batch // bt,),
        in_specs=[
            xp_spec,
            xs_spec,
            pl.BlockSpec((None, None, None, N_SUB, d), lambda i: (l, pi(i) // per_seq, sub, 0, 0)),
            pl.BlockSpec((None, N_SUB, bt, d), lambda i: (l, sub, si(i), 0)),
            pl.BlockSpec((None, None, 1, d), lambda i: (l, sub, 0, 0)),
            full(win), full(cw), full(woa), full(lng), full(wpair), full(sgc), full(sgb), full(wob),
            full(poolw), full(pscale), full(wo),
            pl.BlockSpec((None, n_hc, bt, d_conv), lambda i: (l, 0, si(i), 0)),
            pl.BlockSpec((None, n_hp, bt, d_pool), lambda i: (l, 0, si(i), 0)),
        ],
        out_specs=[
            xp_spec,
            pl.BlockSpec((None, SUBLANES, d_conv), lambda i: (pi(i) // per_seq, 0, 0)),
            pl.BlockSpec((None, MAX_WIN, d_pool), lambda i: (pi(i) // per_seq, 0, 0)),
            xs_spec,
            pl.BlockSpec((n_hc, bt, d_conv), lambda i: (0, si(i), 0)),
            pl.BlockSpec((n_hp, bt, d_pool), lambda i: (0, si(i), 0)),
            pl.BlockSpec((steps, bt, d_gmlp), lambda i: (0, si(i), 0)),
        ],
        out_shape=[
            jax.ShapeDtypeStruct(xp.shape, F32),
            jax.ShapeDtypeStruct((batch, SUBLANES, d_conv), F32),
            jax.ShapeDtypeStruct((batch, MAX_WIN, d_pool), F32),
            jax.ShapeDtypeStruct(xs.shape, F32),
            jax.ShapeDtypeStruct((n_hc, dec_batch, d_conv), F32),
            jax.ShapeDtypeStruct((n_hp, dec_batch, d_pool), F32),
            jax.ShapeDtypeStruct((steps, dec_batch, d_gmlp), F32),
        ],
        scratch_shapes=[
            pltpu.VMEM((d_conv // LANES, SUBLANES + tm, LANES), F32),
            pltpu.VMEM((d_pool // LANES, 2 * MAX_WIN + tm, LANES), F32),
            pltpu.VMEM((n_pool_tmp, 2 * MAX_WIN + tm, LANES), F32),
        ],
        compiler_params=pltpu.CompilerParams(
            dimension_semantics=("arbitrary",), vmem_limit_bytes=VMEM_LIMIT),
        name="mixer",
    )(xp, xs, mod_p, mod_all, ng, win, cw, woa, lng, wpair, sgc, sgb, wob, poolw, pscale, wo,
      hconv, hpool)


def kernel(x_prompt, x_sample, state_conv, state_pool, c_prompt, c_sample, norm_g, w_ada, b_ada,
           w1_gu, w1_dn, w2_gu, w2_dn, w_in, conv_w, w_out_a, ln_g, w_s, b_s, w_out_b,
           pool_w, pool_scale, w_o, final_norm_g):
    depth = w_in.shape[0]
    batch, seq, d = x_prompt.shape
    dec_batch, steps, _ = x_sample.shape
    d_gmlp = ln_g.shape[-1]
    head_dim = d_gmlp // G_HEADS
    assert seq % STEP_TOKENS == 0 and TOKEN_TILE % CHUNK == 0 and TOKEN_TILE % (steps * SUBLANES) == 0
    assert dec_batch % (STEP_TOKENS // steps) == 0 and steps <= CHUNK and 2 * head_dim == LANES

    c_all = jnp.concatenate([c_sample, c_prompt], axis=0)
    mod_all = _ada_call(c_all, w_ada, b_ada)
    mod_p = mod_all[:, :, dec_batch:, :].transpose(0, 2, 1, 3).reshape(depth, batch, N_SUB, 3, d)

    w1gu, w1dn, w2gu, w2dn = w1_gu, w1_dn, w2_gu, w2_dn
    assert w1_gu.shape[1] % GU_STAGE_ROWS == 0 and w1_dn.shape[1] % DN_STAGE_ROWS == 0
    win = w_in.astype(BF16)
    woa = w_out_a.astype(BF16)
    wob = w_out_b.astype(BF16)
    wo = w_o.astype(BF16)
    poolw = pool_w.astype(BF16)
    ng = norm_g.reshape(depth, N_SUB, 1, d)
    lng = ln_g.reshape(depth, 1, d_gmlp)
    pscale = pool_scale.reshape(depth, 1, d)
    fg = final_norm_g.reshape(1, d)
    w_tril = w_s * jnp.tril(jnp.ones((CHUNK, CHUNK), w_s.dtype))
    wpair = w_tril.reshape(depth, G_HEADS // 2, 2, CHUNK, CHUNK).transpose(0, 1, 3, 2, 4)
    wpair = wpair.reshape(depth, G_HEADS // 2, CHUNK, 2 * CHUNK).astype(BF16)
    sgb = jnp.repeat(b_s.transpose(0, 2, 1), head_dim, axis=-1)
    sgc = jnp.repeat(w_tril[:, :, :steps, :steps].transpose(0, 2, 3, 1), head_dim, axis=-1)
    sgc = sgc.reshape(depth, steps * steps, d_gmlp)

    xp = x_prompt
    xs = x_sample.transpose(1, 0, 2)
    hconv = state_conv.transpose(0, 2, 1, 3)
    hpool = state_pool.transpose(0, 2, 1, 3)

    conv_p, conv_s, pool_p, pool_s, v_s = [], [], [], [], []
    for l in range(depth):
        last = l == depth - 1
        xp, xs = _ffn_call(xp, xs, mod_p, mod_all, ng, w1gu, w1dn, fg, l=l, sub=0, final_norm=False)
        xp, ctail, ptail, xs, ncs, nps, vs = _mixer_call(
            xp, xs, mod_p, mod_all, ng, win, conv_w, woa, lng, wpair, sgc, sgb, wob, poolw, pscale,
            wo, hconv, hpool, l=l)
        xp, xs = _ffn_call(xp, xs, mod_p, mod_all, ng, w2gu, w2dn, fg, l=l, sub=2, final_norm=last)
        conv_p.append(ctail[:, SUBLANES - (CONV_W - 1):])
        pool_p.append(ptail[:, 1:])
        conv_s.append(ncs.transpose(1, 0, 2))
        pool_s.append(nps.transpose(1, 0, 2))
        v_s.append(vs.transpose(1, 0, 2))

    return (xp, xs.transpose(1, 0, 2), jnp.stack(conv_p), jnp.stack(conv_s), jnp.stack(pool_p),
            jnp.stack(pool_s), jnp.stack(v_s))
```
